```python
import math
import jax
import jax.numpy as jnp
from jax import lax
import numpy as np

D_MODEL = 4096
BATCH = 4
SEQ = 2048
DEPTH = 1

SSM_HEAD_DIM = 64
SSM_D_INNER = D_MODEL // 2
SSM_HEADS = SSM_D_INNER // SSM_HEAD_DIM
SSM_HEADS_PER_GROUP = 4
SSM_GROUPS = SSM_HEADS // SSM_HEADS_PER_GROUP
SSM_STATE = 128
SSM_CONV = 5
SSM_CHUNK = 128
SSM_CONV_DIM = SSM_D_INNER + 2 * SSM_GROUPS * SSM_STATE
DT_MIN = 0.001
DT_MAX = 0.1

ATTN_PATTERNS = ((128, 1), (512, 4), (2048, 16))
ATTN_HEAD_DIM = 128
ATTN_HEADS_PER_GROUP = D_MODEL // 512
ATTN_HEADS = len(ATTN_PATTERNS) * ATTN_HEADS_PER_GROUP
ATTN_WIDTH = ATTN_HEADS * ATTN_HEAD_DIM
ATTN_OUT_WIDTH = ATTN_HEADS_PER_GROUP * ATTN_HEAD_DIM
ROPE_THETA = 10000.0
NEG_INF = -1e30

N_EXPERTS = 16
EXPERT_FF = D_MODEL // 2
CAPACITY_FACTOR = 2

N_MOD = 6
NORM_EPS = 1e-6

IN_SIZES = (SSM_D_INNER, SSM_CONV_DIM, 2 * SSM_HEADS, ATTN_WIDTH, ATTN_WIDTH, ATTN_WIDTH, 2 * D_MODEL)
IN_DIM = sum(IN_SIZES)

kernel_name = 'hybrid_ssd_dilated_attn_ec_moe_block'


def rms_norm(t, w):
    t32 = t.astype(jnp.float32)
    y = t32 * lax.rsqrt(jnp.mean(t32 * t32, axis=-1, keepdims=True) + NORM_EPS)
    return (y * w.astype(jnp.float32)).astype(t.dtype)


def split_sizes(t, sizes):
    offs = []
    acc = 0
    for s in sizes[:-1]:
        acc += s
        offs.append(acc)
    return jnp.split(t, offs, axis=-1)


def rope(t, pos):
    half = t.shape[-1] // 2
    inv_freq = ROPE_THETA ** (-jnp.arange(half, dtype=jnp.float32) / half)
    ang = pos[:, None] * inv_freq[None, :]
    cos = jnp.cos(ang)[None, :, None, :]
    sin = jnp.sin(ang)[None, :, None, :]
    t32 = t.astype(jnp.float32)
    t1, t2 = t32[..., :half], t32[..., half:]
    return jnp.concatenate([t1 * cos - t2 * sin, t2 * cos + t1 * sin], axis=-1).astype(t.dtype)


def centred_depthwise_conv(t, w, b):
    ch = t.shape[-1]
    width = w.shape[0]
    out = lax.conv_general_dilated(
        t, w[:, None, :].astype(t.dtype), window_strides=(1,),
        padding=((width // 2, width // 2),),
        dimension_numbers=('NWC', 'WIO', 'NWC'), feature_group_count=ch)
    return out + b.astype(t.dtype)


def segsum(a):
    T = a.shape[-1]
    cs = jnp.cumsum(a, axis=-1)
    diff = cs[..., :, None] - cs[..., None, :]
    mask = jnp.tril(jnp.ones((T, T), dtype=bool))
    return jnp.where(mask, diff, -jnp.inf)


def ssd_chunked(xs, dt, a, bm, cm):
    Bsz, S, H, P = xs.shape
    G, N = bm.shape[2], bm.shape[3]
    K = H // G
    T = SSM_CHUNK
    nc = -(-S // T)
    Sp = nc * T
    pad = ((0, 0), (0, Sp - S))
    xd = jnp.pad(xs.astype(jnp.float32) * dt[..., None], pad + ((0, 0), (0, 0)))
    da = jnp.pad(dt * a, pad + ((0, 0),))
    bc = jnp.pad(bm.astype(jnp.float32), pad + ((0, 0), (0, 0))).reshape(Bsz, nc, T, G, N)
    cc = jnp.pad(cm.astype(jnp.float32), pad + ((0, 0), (0, 0))).reshape(Bsz, nc, T, G, N)
    xc = xd.reshape(Bsz, nc, T, G, K, P)
    dac = da.reshape(Bsz, nc, T, G, K).transpose(0, 3, 4, 1, 2)
    a_cs = jnp.cumsum(dac, axis=-1)
    lmat = jnp.exp(segsum(dac))
    cb = jnp.einsum('bclgn,bcsgn->bcgls', cc, bc)
    y_diag = jnp.einsum('bcgls,bgkcls,bcsgkp->bclgkp', cb, lmat, xc)
    decay_states = jnp.exp(a_cs[..., -1:] - a_cs)
    states = jnp.einsum('bclgn,bgkcl,bclgkp->bcgkpn', bc, decay_states, xc)
    states = jnp.concatenate([jnp.zeros_like(states[:, :1]), states], axis=1)
    chunk_decay = jnp.exp(segsum(jnp.pad(a_cs[..., -1], ((0, 0), (0, 0), (0, 0), (1, 0)))))
    states = jnp.einsum('bgkzc,bcgkpn->bzgkpn', chunk_decay, states)[:, :-1]
    y_off = jnp.einsum('bclgn,bcgkpn,bgkcl->bclgkp', cc, states, jnp.exp(a_cs))
    y = (y_diag + y_off).reshape(Bsz, Sp, H, P)[:, :S]
    return y.astype(xs.dtype)


def dilated_window_attention(q, k, v, dilation, radius):
    Bsz, S, H, E = q.shape
    L = -(-S // dilation)
    Sp = L * dilation
    nb = -(-L // radius)
    Lp = nb * radius

    def to_strided(t):
        t = jnp.pad(t, ((0, 0), (0, Sp - S), (0, 0), (0, 0)))
        t = t.reshape(Bsz, L, dilation, H, E).transpose(0, 2, 3, 1, 4)
        return jnp.pad(t, ((0, 0), (0, 0), (0, 0), (0, Lp - L), (0, 0)))

    def band(t):
        t = jnp.pad(t, ((0, 0), (0, 0), (0, 0), (radius, radius), (0, 0)))
        t = t.reshape(Bsz, dilation, H, nb + 2, radius, E)
        return jnp.concatenate([t[:, :, :, :-2], t[:, :, :, 1:-1], t[:, :, :, 2:]], axis=4)

    qb = to_strided(q).reshape(Bsz, dilation, H, nb, radius, E)
    kb = band(to_strided(k))
    vb = band(to_strided(v))
    scores = jnp.einsum('bdhnqe,bdhnke->bdhnqk', qb, kb).astype(jnp.float32) * (E ** -0.5)
    t_q = jnp.arange(nb)[:, None] * radius + jnp.arange(radius)[None, :]
    t_k = (jnp.arange(nb)[:, None] - 1) * radius + jnp.arange(3 * radius)[None, :]
    r_idx = jnp.arange(dilation)[:, None, None, None]
    valid = ((jnp.abs(t_k[:, None, :] - t_q[:, :, None]) <= radius)[None]
             & (t_k >= 0)[None, :, None, :]
             & (t_k[None, :, None, :] * dilation + r_idx < S))
    scores = jnp.where(valid[None, :, None], scores, NEG_INF)
    m = jnp.max(scores, axis=-1, keepdims=True)
    p = jnp.exp(scores - m)
    s = jnp.sum(p, axis=-1)
    o = jnp.einsum('bdhnqk,bdhnke->bdhnqe', p, vb.astype(jnp.float32)) / s[..., None]
    lse = m[..., 0] + jnp.log(s)
    o = o.reshape(Bsz, dilation, H, Lp, E)[:, :, :, :L].transpose(0, 3, 1, 2, 4)
    o = o.reshape(Bsz, Sp, H, E)[:, :S]
    lse = lse.reshape(Bsz, dilation, H, Lp)[:, :, :, :L].transpose(0, 3, 1, 2)
    lse = lse.reshape(Bsz, Sp, H)[:, :S]
    return o, lse


def hybrid_mixer(h, w_in, conv_w, conv_b, dt_bias_f, dt_bias_b, a_log_f, a_log_b, d_skip,
                 ssm_norm_w, w_ssm_out, w_attn_out, w_o):
    Bsz, S, _ = h.shape
    proj = jnp.einsum('bsd,de->bse', h, w_in)
    z, xbc, dt_raw, q, k, v, gates = split_sizes(proj, IN_SIZES)

    xbc = jax.nn.silu(centred_depthwise_conv(xbc, conv_w, conv_b))
    xs, b_ssm, c_ssm = split_sizes(xbc, (SSM_D_INNER, SSM_GROUPS * SSM_STATE, SSM_GROUPS * SSM_STATE))
    xs = xs.reshape(Bsz, S, SSM_HEADS, SSM_HEAD_DIM)
    b_ssm = b_ssm.reshape(Bsz, S, SSM_GROUPS, SSM_STATE)
    c_ssm = c_ssm.reshape(Bsz, S, SSM_GROUPS, SSM_STATE)
    dt_f, dt_b = jnp.split(dt_raw.astype(jnp.float32), 2, axis=-1)
    dt_f = jax.nn.softplus(dt_f + dt_bias_f.astype(jnp.float32))
    dt_b = jax.nn.softplus(dt_b + dt_bias_b.astype(jnp.float32))
    a_f = -jnp.exp(a_log_f.astype(jnp.float32))
    a_b = -jnp.exp(a_log_b.astype(jnp.float32))
    flip = lambda t: jnp.flip(t, axis=1)
    y_f = ssd_chunked(xs, dt_f, a_f, b_ssm, c_ssm)
    y_b = flip(ssd_chunked(flip(xs), flip(dt_b), a_b, flip(b_ssm), flip(c_ssm)))
    y = y_f + y_b + d_skip.astype(xs.dtype)[:, None] * xs
    y = y.reshape(Bsz, S, SSM_D_INNER) * jax.nn.silu(z)
    y_ssm = jnp.einsum('bse,ed->bsd', rms_norm(y, ssm_norm_w), w_ssm_out)

    pos = jnp.arange(S, dtype=jnp.float32)
    q = rope(q.reshape(Bsz, S, ATTN_HEADS, ATTN_HEAD_DIM), pos)
    k = rope(k.reshape(Bsz, S, ATTN_HEADS, ATTN_HEAD_DIM), pos)
    v = v.reshape(Bsz, S, ATTN_HEADS, ATTN_HEAD_DIM)
    outs, lses = [], []
    for g, (window, dilation) in enumerate(ATTN_PATTERNS):
        sl = slice(g * ATTN_HEADS_PER_GROUP, (g + 1) * ATTN_HEADS_PER_GROUP)
        o_g, lse_g = dilated_window_attention(q[:, :, sl], k[:, :, sl], v[:, :, sl],
                                              dilation, window // (2 * dilation))
        outs.append(o_g)
        lses.append(lse_g)
    mix = jax.nn.softmax(jnp.stack(lses, axis=0), axis=0)
    o = jnp.einsum('gbsh,gbshe->bshe', mix, jnp.stack(outs, axis=0)).astype(h.dtype)
    y_att = jnp.einsum('bse,ed->bsd', o.reshape(Bsz, S, ATTN_OUT_WIDTH), w_attn_out)

    g_ssm, g_att = jnp.split(jax.nn.sigmoid(gates), 2, axis=-1)
    merged = g_ssm * y_ssm + g_att * y_att
    return jnp.einsum('bsd,de->bse', merged, w_o)


def expert_choice_moe(h, w_router, w_gate_e, w_up_e, w_down_e):
    Bsz, S, D = h.shape
    cap = max(1, min(S, CAPACITY_FACTOR * S // N_EXPERTS))
    logits = jnp.einsum('bsd,de->bse', h, w_router).astype(jnp.float32)
    aff = jax.nn.softmax(logits, axis=-1)
    vals, idx = lax.top_k(jnp.swapaxes(aff, 1, 2), cap)
    xe = jax.vmap(lambda hb, ib: hb[ib])(h, idx)
    gate = jnp.einsum('becd,edf->becf', xe, w_gate_e)
    up = jnp.einsum('becd,edf->becf', xe, w_up_e)
    out = jnp.einsum('becf,efd->becd', jax.nn.silu(gate) * up, w_down_e)
    out = out * vals.astype(out.dtype)[..., None]
    return jax.vmap(lambda ib, ob: jnp.zeros((S, D), ob.dtype).at[ib.reshape(-1)].add(ob.reshape(-1, D)))(idx, out)


def setup_inputs(seed: int = 0) -> dict:
    key = jax.random.key(seed)
    ks = jax.random.split(key, 32)
    D = D_MODEL
    f32 = jnp.float32

    def nrm(k, shape, fan_in):
        return jax.random.normal(k, shape, f32) * (fan_in ** -0.5)

    def gain(k, shape):
        return 1.0 + 0.02 * jax.random.normal(k, shape, f32)

    def dt_bias(k):
        u = jax.random.uniform(k, (DEPTH, SSM_HEADS), f32)
        dt = jnp.exp(u * (math.log(DT_MAX) - math.log(DT_MIN)) + math.log(DT_MIN))
        return dt + jnp.log(-jnp.expm1(-dt))

    def a_log(k):
        return jnp.log(jax.random.uniform(k, (DEPTH, SSM_HEADS), f32, minval=1.0, maxval=16.0))

    return {
        'x': jax.random.normal(ks[0], (BATCH, SEQ, D), f32),
        'c': jax.random.normal(ks[1], (BATCH, D), f32),
        'norm1_w': gain(ks[2], (DEPTH, D)),
        'norm2_w': gain(ks[3], (DEPTH, D)),
        'normf_w': gain(ks[4], (D,)),
        'w_ada': nrm(ks[5], (DEPTH, D, N_MOD * D), D),
        'b_ada': 0.02 * jax.random.normal(ks[6], (DEPTH, N_MOD * D), f32),
        'w_in': nrm(ks[7], (DEPTH, D, IN_DIM), D),
        'conv_w': nrm(ks[8], (DEPTH, SSM_CONV, SSM_CONV_DIM), SSM_CONV),
        'conv_b': 0.02 * jax.random.normal(ks[9], (DEPTH, SSM_CONV_DIM), f32),
        'dt_bias_f': dt_bias(ks[10]),
        'dt_bias_b': dt_bias(ks[11]),
        'a_log_f': a_log(ks[12]),
        'a_log_b': a_log(ks[13]),
        'd_skip': 1.0 + 0.1 * jax.random.normal(ks[14], (DEPTH, SSM_HEADS), f32),
        'ssm_norm_w': gain(ks[15], (DEPTH, SSM_D_INNER)),
        'w_ssm_out': nrm(ks[16], (DEPTH, SSM_D_INNER, D), SSM_D_INNER),
        'w_attn_out': nrm(ks[17], (DEPTH, ATTN_OUT_WIDTH, D), ATTN_OUT_WIDTH),
        'w_o': nrm(ks[18], (DEPTH, D, D), D),
        'w_router': nrm(ks[19], (DEPTH, D, N_EXPERTS), D),
        'w_gate_e': nrm(ks[20], (DEPTH, N_EXPERTS, D, EXPERT_FF), D),
        'w_up_e': nrm(ks[21], (DEPTH, N_EXPERTS, D, EXPERT_FF), D),
        'w_down_e': nrm(ks[22], (DEPTH, N_EXPERTS, EXPERT_FF, D), EXPERT_FF),
    }


def reference(x, c, norm1_w, norm2_w, normf_w, w_ada, b_ada, w_in, conv_w, conv_b,
              dt_bias_f, dt_bias_b, a_log_f, a_log_b, d_skip, ssm_norm_w, w_ssm_out,
              w_attn_out, w_o, w_router, w_gate_e, w_up_e, w_down_e):
    c_act = jax.nn.silu(c)
    for layer in range(DEPTH):
        mod = jnp.einsum('bd,de->be', c_act, w_ada[layer]) + b_ada[layer]
        shift1, scale1, gate1, shift2, scale2, gate2 = [m[:, None, :] for m in jnp.split(mod, N_MOD, axis=-1)]
        h = rms_norm(x, norm1_w[layer]) * (1.0 + scale1) + shift1
        mixed = hybrid_mixer(h, w_in[layer], conv_w[layer], conv_b[layer], dt_bias_f[layer],
                             dt_bias_b[layer], a_log_f[layer], a_log_b[layer], d_skip[layer],
                             ssm_norm_w[layer], w_ssm_out[layer], w_attn_out[layer], w_o[layer])
        x = x + gate1 * mixed
        h = rms_norm(x, norm2_w[layer]) * (1.0 + scale2) + shift2
        x = x + gate2 * expert_choice_moe(h, w_router[layer], w_gate_e[layer], w_up_e[layer], w_down_e[layer])
    return rms_norm(x, normf_w)
```

```python
import functools
import math

import jax
import jax.numpy as jnp
from jax import lax
from jax.experimental import pallas as pl
from jax.experimental.pallas import tpu as pltpu

F32 = jnp.float32
BF16 = jnp.bfloat16

D = 4096
B = 4
S = 2048
T = B * S
HEAD_P = 64
D_INNER = D // 2
N_HEADS = D_INNER // HEAD_P
HEADS_PER_GROUP = 4
N_GROUPS = N_HEADS // HEADS_PER_GROUP
N_STATE = 128
CONV_K = 5
CHUNK = 128
N_CHUNKS = S // CHUNK
CONV_DIM = D_INNER + 2 * N_GROUPS * N_STATE
PATTERNS = ((128, 1), (512, 4), (2048, 16))
A_HEAD = 128
A_HPG = D // 512
A_WIDTH = len(PATTERNS) * A_HPG * A_HEAD
A_OUT = A_HPG * A_HEAD
ROPE_THETA = 10000.0
NEG_INF = -1e30
N_EXPERTS = 16
FF = D // 2
CAP = 2 * S // N_EXPERTS
EPS = 1e-6
IN_SIZES = (D_INNER, CONV_DIM, 2 * N_HEADS, A_WIDTH, A_WIDTH, A_WIDTH, 2 * D)
OFF_DT = D_INNER + CONV_DIM
OFF_Q = OFF_DT + 2 * N_HEADS
N_QKVG = 3 * A_WIDTH + 2 * D
VG_W = A_WIDTH + 2 * D
LANE = 128


def _cp(sem, vmem_mb=None):
    kw = dict(dimension_semantics=sem)
    if vmem_mb is not None:
        kw["vmem_limit_bytes"] = vmem_mb << 20
    return pltpu.CompilerParams(**kw)


def _dot(a, b):
    return jnp.dot(a, b, preferred_element_type=F32)


def _dot_nt(a, b):
    return lax.dot_general(a, b, (((1,), (1,)), ((), ())), preferred_element_type=F32)


def _split2(v):
    h = v.astype(BF16)
    l = (v - h.astype(F32)).astype(BF16)
    return h, l


def _split3(v):
    h = v.astype(BF16)
    r = v - h.astype(F32)
    m = r.astype(BF16)
    l = (r - m.astype(F32)).astype(BF16)
    return h, m, l


def _silu(v):
    return v * jax.nn.sigmoid(v)


def _softplus(v):
    return jnp.maximum(v, 0.0) + jnp.log1p(jnp.exp(-jnp.abs(v)))


def _ada_kernel(c_ref, w_ref, b_ref, o_ref):
    ca = _silu(c_ref[...]).astype(BF16)
    o_ref[...] = _dot(ca, w_ref[...].astype(BF16)) + b_ref[...]


def _ada(c8, w, b):
    n = w.shape[1]
    tn = 512
    return pl.pallas_call(
        _ada_kernel,
        grid=(n // tn,),
        in_specs=[pl.BlockSpec((8, D), lambda j: (0, 0)),
                  pl.BlockSpec((D, tn), lambda j: (0, j)),
                  pl.BlockSpec((1, tn), lambda j: (0, j))],
        out_specs=pl.BlockSpec((8, tn), lambda j: (0, j)),
        out_shape=jax.ShapeDtypeStruct((8, n), F32),
        compiler_params=_cp(("arbitrary",), 40),
        name="ada_mod",
    )(c8, w, b)


def _norm_mod_kernel(x_ref, w_ref, sc_ref, sh_ref, o_ref):
    x = x_ref[...]
    ms = jnp.mean(x * x, axis=-1, keepdims=True)
    y = x * lax.rsqrt(ms + EPS) * w_ref[...]
    o_ref[...] = (y * (1.0 + sc_ref[0]) + sh_ref[0]).astype(o_ref.dtype)


def _norm_mod(x2d, w, sc, sh):
    tm = 256
    per_b = S // tm
    return pl.pallas_call(
        _norm_mod_kernel,
        grid=(T // tm,),
        in_specs=[pl.BlockSpec((tm, D), lambda i: (i, 0)),
                  pl.BlockSpec((1, D), lambda i: (0, 0)),
                  pl.BlockSpec((1, 1, D), lambda i: (i // per_b, 0, 0)),
                  pl.BlockSpec((1, 1, D), lambda i: (i // per_b, 0, 0))],
        out_specs=pl.BlockSpec((tm, D), lambda i: (i, 0)),
        out_shape=jax.ShapeDtypeStruct((T, D), BF16),
        compiler_params=_cp(("arbitrary",), 40),
        name="norm1_mod",
    )(x2d, w, sc, sh)


def _mm_kernel(a_ref, w_ref, o_ref, *scratch, cast_w):
    if cast_w:
        (wb_ref,) = scratch

        @pl.when(pl.program_id(1) == 0)
        def _():
            wb_ref[...] = w_ref[...].astype(BF16)

        w = wb_ref[...]
    else:
        w = w_ref[...]
    o_ref[...] = _dot(a_ref[...], w).astype(o_ref.dtype)


def _mm(a, w, *, col0, n, tm, tn, out_dtype, vmem_mb, name):
    m, k = a.shape
    cast_w = w.dtype != BF16
    cb = col0 // tn
    scratch = [pltpu.VMEM((k, tn), BF16)] if cast_w else []
    return pl.pallas_call(
        functools.partial(_mm_kernel, cast_w=cast_w),
        grid=(n // tn, m // tm),
        in_specs=[pl.BlockSpec((tm, k), lambda j, i: (i, 0)),
                  pl.BlockSpec((k, tn), lambda j, i: (0, cb + j))],
        out_specs=pl.BlockSpec((tm, tn), lambda j, i: (i, j)),
        out_shape=jax.ShapeDtypeStruct((m, n), out_dtype),
        scratch_shapes=scratch,
        compiler_params=_cp(("arbitrary", "arbitrary"), vmem_mb),
        name=name,
    )(a, w)


def _mm_rope_kernel(a_ref, w_ref, cos_ref, sin_ref, o_ref):
    acc = _dot(a_ref[...], w_ref[...])
    cos = cos_ref[0]
    sin = sin_ref[0]
    for h in range(acc.shape[1] // A_HEAD):
        t = acc[:, h * A_HEAD:(h + 1) * A_HEAD]
        rot = t * cos + pltpu.roll(t, A_HEAD // 2, 1) * sin
        o_ref[:, h * A_HEAD:(h + 1) * A_HEAD] = rot.astype(o_ref.dtype)


def _mm_rope(a, w, cos_tab, sin_tab, *, n, tm, tn):
    m, k = a.shape
    q_tiles = A_WIDTH // tn
    per_b = S // tm
    return pl.pallas_call(
        _mm_rope_kernel,
        grid=(n // tn, m // tm),
        in_specs=[pl.BlockSpec((tm, k), lambda j, i: (i, 0)),
                  pl.BlockSpec((k, tn), lambda j, i: (0, j)),
                  pl.BlockSpec((1, tm, A_HEAD), lambda j, i: (j // q_tiles, i % per_b, 0)),
                  pl.BlockSpec((1, tm, A_HEAD), lambda j, i: (j // q_tiles, i % per_b, 0))],
        out_specs=pl.BlockSpec((tm, tn), lambda j, i: (i, j)),
        out_shape=jax.ShapeDtypeStruct((m, n), BF16),
        compiler_params=_cp(("arbitrary", "arbitrary"), 48),
        name="proj_qk_rope",
    )(a, w, cos_tab, sin_tab)


def _dt_kernel(a_ref, w_ref, wt_ref, o_ref, ot_ref):
    a = a_ref[...]
    o_ref[...] = _dot(a, w_ref[...].astype(BF16))
    ot_ref[...] = _dot_nt(wt_ref[...].astype(BF16), a)


def _dt_proj(a, w_in, w_dt_t):
    tm = 1024
    return pl.pallas_call(
        _dt_kernel,
        grid=(T // tm,),
        in_specs=[pl.BlockSpec((tm, D), lambda i: (i, 0)),
                  pl.BlockSpec((D, LANE), lambda i: (0, OFF_DT // LANE)),
                  pl.BlockSpec((2 * N_HEADS, D), lambda i: (0, 0))],
        out_specs=[pl.BlockSpec((tm, LANE), lambda i: (i, 0)),
                   pl.BlockSpec((2 * N_HEADS, tm), lambda i: (0, i))],
        out_shape=[jax.ShapeDtypeStruct((T, LANE), F32),
                   jax.ShapeDtypeStruct((2 * N_HEADS, T), F32)],
        compiler_params=_cp(("arbitrary",), 40),
        name="proj_dt",
    )(a, w_in, w_dt_t)


def _conv_kernel(x_ref, w_ref, b_ref, o_ref):
    x = x_ref[0].astype(F32)
    n = x.shape[0]
    row = lax.broadcasted_iota(jnp.int32, x.shape, 0)
    acc = x * w_ref[CONV_K // 2:CONV_K // 2 + 1, :] + b_ref[...]
    for k in range(CONV_K):
        off = k - CONV_K // 2
        if off == 0:
            continue
        shifted = pltpu.roll(x, (-off) % n, 0)
        ok = (row + off >= 0) & (row + off < n)
        acc = acc + jnp.where(ok, shifted, 0.0) * w_ref[k:k + 1, :]
    o_ref[0] = _silu(acc).astype(o_ref.dtype)


def _conv_silu(zx3, conv_w, conv_b):
    tc = 512
    c0 = D_INNER // tc
    return pl.pallas_call(
        _conv_kernel,
        grid=(B, CONV_DIM // tc),
        in_specs=[pl.BlockSpec((1, S, tc), lambda b, j: (b, 0, c0 + j)),
                  pl.BlockSpec((CONV_K, tc), lambda b, j: (0, j)),
                  pl.BlockSpec((1, tc), lambda b, j: (0, j))],
        out_specs=pl.BlockSpec((1, S, tc), lambda b, j: (b, 0, j)),
        out_shape=jax.ShapeDtypeStruct((B, S, CONV_DIM), BF16),
        compiler_params=_cp(("arbitrary", "arbitrary"), 48),
        name="conv_silu",
    )(zx3, conv_w, conv_b)


def _ssd_direction(x_ref, y_ref, st_ref, dt_raw, dtT_raw, bias, biasT, alog, alogT, e_ref, dsk, reverse):
    tc = CHUNK
    gw = HEADS_PER_GROUP * HEAD_P
    dt = _softplus(dt_raw + bias)
    dtT = _softplus(dtT_raw + biasT)
    da = dt * (-jnp.exp(alog))
    daT = dtT * (-jnp.exp(alogT))
    r = lax.broadcasted_iota(jnp.int32, (tc, tc), 0)
    c = lax.broadcasted_iota(jnp.int32, (tc, tc), 1)
    keep = (c >= r) if reverse else (c <= r)
    tri = jnp.where(keep, 1.0, 0.0).astype(BF16)
    triT = jnp.where((r >= c) if reverse else (r <= c), 1.0, 0.0).astype(BF16)
    h3 = _split3(da)
    cs = _dot(tri, h3[0]) + _dot(tri, h3[1]) + _dot(tri, h3[2])
    t3 = _split3(daT)
    csT = _dot(t3[0], triT) + _dot(t3[1], triT) + _dot(t3[2], triT)
    tot = cs[0:1, :] if reverse else cs[tc - 1:tc, :]
    dth, dtl = _split2(dt)
    edh, edl = _split2(jnp.exp(tot - cs))
    ech, ecl = _split2(jnp.exp(cs))
    eth, etl = _split2(jnp.broadcast_to(jnp.exp(tot), (16, N_HEADS)))
    lanehead = lax.shift_right_logical(lax.broadcasted_iota(jnp.int32, (tc, gw), 1), HEAD_P.bit_length() - 1)
    for g in range(N_GROUPS):
        eg = e_ref[:, g * gw:(g + 1) * gw]
        dt_e = _dot(dth, eg) + _dot(dtl, eg)
        edec_e = _dot(edh, eg) + _dot(edl, eg)
        ecs_e = _dot(ech, eg) + _dot(ecl, eg)
        etot_e = (_dot(eth, eg) + _dot(etl, eg))[0:1, :]
        xs = x_ref[:, g * gw:(g + 1) * gw].astype(F32)
        bg = x_ref[:, D_INNER + g * N_STATE:D_INNER + (g + 1) * N_STATE]
        cg = x_ref[:, D_INNER + N_GROUPS * N_STATE + g * N_STATE:D_INNER + N_GROUPS * N_STATE + (g + 1) * N_STATE]
        xd = xs * dt_e
        xdd = (xd * edec_e).astype(BF16)
        cb = _dot_nt(cg, bg)
        ms = []
        rs = []
        for k in range(HEADS_PER_GROUP):
            h = HEADS_PER_GROUP * g + k
            decay = jnp.where(keep, jnp.exp(cs[:, h:h + 1] - csT[h:h + 1, :]), 0.0)
            ms.append((cb * decay).astype(BF16))
            rs.append(jnp.where(lanehead == k, xd, 0.0).astype(BF16))
        lhs = jnp.concatenate(ms, axis=1)
        rhs = jnp.concatenate(rs, axis=0)
        sg = st_ref[g]
        y = _dot(lhs, rhs) + _dot(cg, sg.astype(BF16)) * ecs_e
        if dsk is not None:
            y = y + xs * dsk[:, g * gw:(g + 1) * gw]
        y_ref[:, g * gw:(g + 1) * gw] = y
        bgT = bg.astype(F32).T.astype(BF16)
        st_ref[g] = sg * etot_e + _dot(bgT, xdd)


def _ssd_kernel(xf_ref, xb_ref, dtf_ref, dtb_ref, dtTf_ref, dtTb_ref, bias_ref, biasT_ref,
                alog_ref, alogT_ref, dsk_ref, e_ref, yf_ref, yb_ref, sf_ref, sb_ref):
    @pl.when(pl.program_id(1) == 0)
    def _():
        sf_ref[...] = jnp.zeros_like(sf_ref)
        sb_ref[...] = jnp.zeros_like(sb_ref)

    nh = N_HEADS
    _ssd_direction(xf_ref, yf_ref, sf_ref, dtf_ref[:, 0:nh], dtTf_ref[0:nh, :], bias_ref[:, 0:nh],
                   biasT_ref[0:nh, :], alog_ref[:, 0:nh], alogT_ref[0:nh, :], e_ref, dsk_ref[...], False)
    _ssd_direction(xb_ref, yb_ref, sb_ref, dtb_ref[:, nh:2 * nh], dtTb_ref[nh:2 * nh, :], bias_ref[:, nh:2 * nh],
                   biasT_ref[nh:2 * nh, :], alog_ref[:, nh:2 * nh], alogT_ref[nh:2 * nh, :], e_ref, None, True)


def _ssd(xbc, dt, dtT, bias, biasT, alog, alogT, dsk, expand):
    nc = N_CHUNKS
    fwd = lambda b, c: (b * nc + c, 0)
    bwd = lambda b, c: (b * nc + nc - 1 - c, 0)
    fwdT = lambda b, c: (0, b * nc + c)
    bwdT = lambda b, c: (0, b * nc + nc - 1 - c)
    const = lambda b, c: (0, 0)
    gw = HEADS_PER_GROUP * HEAD_P
    return pl.pallas_call(
        _ssd_kernel,
        grid=(B, nc),
        in_specs=[pl.BlockSpec((CHUNK, CONV_DIM), fwd),
                  pl.BlockSpec((CHUNK, CONV_DIM), bwd),
                  pl.BlockSpec((CHUNK, LANE), fwd),
                  pl.BlockSpec((CHUNK, LANE), bwd),
                  pl.BlockSpec((2 * N_HEADS, CHUNK), fwdT),
                  pl.BlockSpec((2 * N_HEADS, CHUNK), bwdT),
                  pl.BlockSpec((1, 2 * N_HEADS), const),
                  pl.BlockSpec((2 * N_HEADS, 1), const),
                  pl.BlockSpec((1, 2 * N_HEADS), const),
                  pl.BlockSpec((2 * N_HEADS, 1), const),
                  pl.BlockSpec((1, D_INNER), const),
                  pl.BlockSpec((N_HEADS, D_INNER), const)],
        out_specs=[pl.BlockSpec((CHUNK, D_INNER), fwd),
                   pl.BlockSpec((CHUNK, D_INNER), bwd)],
        out_shape=[jax.ShapeDtypeStruct((T, D_INNER), F32),
                   jax.ShapeDtypeStruct((T, D_INNER), F32)],
        scratch_shapes=[pltpu.VMEM((N_GROUPS, N_STATE, gw), F32),
                        pltpu.VMEM((N_GROUPS, N_STATE, gw), F32)],
        compiler_params=_cp(("arbitrary", "arbitrary"), 40),
        name="ssd_scan",
    )(xbc, xbc, dt, dt, dtT, dtT, bias, biasT, alog, alogT, dsk, expand)


def _gnorm_kernel(yf_ref, yb_ref, z_ref, w_ref, o_ref):
    z = z_ref[...].astype(F32)
    y = (yf_ref[...] + yb_ref[...]) * _silu(z)
    ms = jnp.mean(y * y, axis=-1, keepdims=True)
    o_ref[...] = (y * lax.rsqrt(ms + EPS) * w_ref[...]).astype(o_ref.dtype)


def _gated_norm(yf, yb, zx, w):
    tm = 512
    return pl.pallas_call(
        _gnorm_kernel,
        grid=(T // tm,),
        in_specs=[pl.BlockSpec((tm, D_INNER), lambda i: (i, 0)),
                  pl.BlockSpec((tm, D_INNER), lambda i: (i, 0)),
                  pl.BlockSpec((tm, D_INNER), lambda i: (i, 0)),
                  pl.BlockSpec((1, D_INNER), lambda i: (0, 0))],
        out_specs=pl.BlockSpec((tm, D_INNER), lambda i: (i, 0)),
        out_shape=jax.ShapeDtypeStruct((T, D_INNER), BF16),
        compiler_params=_cp(("arbitrary",), 40),
        name="ssm_gated_norm",
    )(yf, yb, zx, w)


def _attn_kernel(q_ref, k_ref, v_ref, o_ref, lse_ref, *, length, hb, radius):
    tq = 128
    win = min(length, tq + 2 * radius)
    for h in range(hb):
        cols = slice(h * A_HEAD, (h + 1) * A_HEAD)
        for qb in range(length // tq):
            q0 = qb * tq
            ws = min(max(q0 - radius, 0), length - win)
            q = q_ref[0, q0:q0 + tq, cols]
            k = k_ref[0, ws:ws + win, cols]
            v = v_ref[0, ws:ws + win, cols]
            s = _dot_nt(q, k)
            ti = q0 + lax.broadcasted_iota(jnp.int32, (tq, win), 0)
            tj = ws + lax.broadcasted_iota(jnp.int32, (tq, win), 1)
            s = jnp.where((tj - ti <= radius) & (ti - tj <= radius), s, NEG_INF)
            m = jnp.max(s, axis=-1, keepdims=True)
            p = jnp.exp(s - m)
            l = jnp.sum(p, axis=-1, keepdims=True)
            o = _dot(p.astype(BF16), v) / l
            o_ref[0, q0:q0 + tq, cols] = o.astype(o_ref.dtype)
            lse_ref[0, 0, 0, q0:q0 + tq, h:h + 1] = m + jnp.log(l)


def _attention_group(qk3, vg3, g, dil, hb):
    length = S // dil
    window = PATTERNS[g][0]
    radius = window // (2 * dil)
    bw = hb * A_HEAD
    nj = A_HPG // hb
    qk_w = 2 * A_WIDTH
    q_map = lambda b, r, j: (b, 0, (r * qk_w + g * A_OUT) // bw + j)
    k_map = lambda b, r, j: (b, 0, (r * qk_w + A_WIDTH + g * A_OUT) // bw + j)
    v_map = lambda b, r, j: (b, 0, (r * VG_W + g * A_OUT) // bw + j)
    o_map = lambda b, r, j: (b, 0, (r * A_OUT) // bw + j)
    return pl.pallas_call(
        functools.partial(_attn_kernel, length=length, hb=hb, radius=radius),
        grid=(B, dil, nj),
        in_specs=[pl.BlockSpec((1, length, bw), q_map),
                  pl.BlockSpec((1, length, bw), k_map),
                  pl.BlockSpec((1, length, bw), v_map)],
        out_specs=[pl.BlockSpec((1, length, bw), o_map),
                   pl.BlockSpec((1, 1, 1, length, hb), lambda b, r, j: (b, r, j, 0, 0))],
        out_shape=[jax.ShapeDtypeStruct((B, length, dil * A_OUT), BF16),
                   jax.ShapeDtypeStruct((B, dil, nj, length, hb), F32)],
        compiler_params=_cp(("arbitrary", "arbitrary", "arbitrary"), 40),
        name=f"attn_dil{dil}",
    )(qk3, qk3, vg3)


def _combine_kernel(o0_ref, o1_ref, o2_ref, lse_ref, out_ref):
    lse = lse_ref[...]
    for hh in range(A_HPG):
        l0 = lse[:, hh:hh + 1]
        l1 = lse[:, A_HPG + hh:A_HPG + hh + 1]
        l2 = lse[:, 2 * A_HPG + hh:2 * A_HPG + hh + 1]
        m = jnp.maximum(jnp.maximum(l0, l1), l2)
        e0 = jnp.exp(l0 - m)
        e1 = jnp.exp(l1 - m)
        e2 = jnp.exp(l2 - m)
        inv = 1.0 / (e0 + e1 + e2)
        cols = slice(hh * A_HEAD, (hh + 1) * A_HEAD)
        mix = ((e0 * inv) * o0_ref[:, cols].astype(F32) + (e1 * inv) * o1_ref[:, cols].astype(F32)
               + (e2 * inv) * o2_ref[:, cols].astype(F32))
        out_ref[:, cols] = mix.astype(out_ref.dtype)


def _combine_groups(o0, o1, o2, lse):
    tm = 512
    n_l = lse.shape[1]
    return pl.pallas_call(
        _combine_kernel,
        grid=(T // tm,),
        in_specs=[pl.BlockSpec((tm, A_OUT), lambda i: (i, 0)),
                  pl.BlockSpec((tm, A_OUT), lambda i: (i, 0)),
                  pl.BlockSpec((tm, A_OUT), lambda i: (i, 0)),
                  pl.BlockSpec((tm, n_l), lambda i: (i, 0))],
        out_specs=pl.BlockSpec((tm, A_OUT), lambda i: (i, 0)),
        out_shape=jax.ShapeDtypeStruct((T, A_OUT), BF16),
        compiler_params=_cp(("arbitrary",), 40),
        name="attn_combine",
    )(o0, o1, o2, lse)


def _merge_kernel(ys_ref, oa_ref, ws_ref, wa_ref, gs_ref, ga_ref, o_ref, wsb_ref, wab_ref):
    @pl.when(pl.program_id(1) == 0)
    def _():
        wsb_ref[...] = ws_ref[...].astype(BF16)
        wab_ref[...] = wa_ref[...].astype(BF16)

    y_ssm = _dot(ys_ref[...], wsb_ref[...])
    y_att = _dot(oa_ref[...], wab_ref[...])
    g_s = jax.nn.sigmoid(gs_ref[...].astype(F32))
    g_a = jax.nn.sigmoid(ga_ref[...].astype(F32))
    o_ref[...] = (g_s * y_ssm + g_a * y_att).astype(o_ref.dtype)


def _merge(yn, oc, w_ssm_out, w_attn_out, vg):
    tm, tn = 1024, 512
    g0 = A_WIDTH // tn
    g1 = (A_WIDTH + D) // tn
    return pl.pallas_call(
        _merge_kernel,
        grid=(D // tn, T // tm),
        in_specs=[pl.BlockSpec((tm, D_INNER), lambda j, i: (i, 0)),
                  pl.BlockSpec((tm, A_OUT), lambda j, i: (i, 0)),
                  pl.BlockSpec((D_INNER, tn), lambda j, i: (0, j)),
                  pl.BlockSpec((A_OUT, tn), lambda j, i: (0, j)),
                  pl.BlockSpec((tm, tn), lambda j, i: (i, g0 + j)),
                  pl.BlockSpec((tm, tn), lambda j, i: (i, g1 + j))],
        out_specs=pl.BlockSpec((tm, tn), lambda j, i: (i, j)),
        out_shape=jax.ShapeDtypeStruct((T, D), BF16),
        scratch_shapes=[pltpu.VMEM((D_INNER, tn), BF16), pltpu.VMEM((A_OUT, tn), BF16)],
        compiler_params=_cp(("arbitrary", "arbitrary"), 48),
        name="branch_merge",
    )(yn, oc, w_ssm_out, w_attn_out, vg, vg)


def _wo_kernel(a_ref, w_ref, x_ref, g_ref, o_ref, wb_ref):
    @pl.when(pl.program_id(1) == 0)
    def _():
        wb_ref[...] = w_ref[...].astype(BF16)

    o_ref[...] = x_ref[...] + g_ref[0] * _dot(a_ref[...], wb_ref[...])


def _wo_residual(merged, w_o, x2d, gate1):
    tm, tn = 1024, 512
    per_b = S // tm
    return pl.pallas_call(
        _wo_kernel,
        grid=(D // tn, T // tm),
        in_specs=[pl.BlockSpec((tm, D), lambda j, i: (i, 0)),
                  pl.BlockSpec((D, tn), lambda j, i: (0, j)),
                  pl.BlockSpec((tm, tn), lambda j, i: (i, j)),
                  pl.BlockSpec((1, 1, tn), lambda j, i: (i // per_b, 0, j))],
        out_specs=pl.BlockSpec((tm, tn), lambda j, i: (i, j)),
        out_shape=jax.ShapeDtypeStruct((T, D), F32),
        scratch_shapes=[pltpu.VMEM((D, tn), BF16)],
        compiler_params=_cp(("arbitrary", "arbitrary"), 52),
        name="out_proj_residual",
    )(merged, w_o, x2d, gate1)


def _norm_router_kernel(x_ref, w_ref, sc_ref, sh_ref, wr_ref, h_ref, aff_ref):
    x = x_ref[...]
    ms = jnp.mean(x * x, axis=-1, keepdims=True)
    h = x * lax.rsqrt(ms + EPS) * w_ref[...] * (1.0 + sc_ref[0]) + sh_ref[0]
    h_ref[...] = h.astype(h_ref.dtype)
    hh, hl = _split2(h)
    wh, wl = _split2(wr_ref[...])
    logits = _dot(hh, wh) + _dot(hl, wh) + _dot(hh, wl)
    lane = lax.broadcasted_iota(jnp.int32, logits.shape, 1)
    logits = jnp.where(lane < N_EXPERTS, logits, NEG_INF)
    m = jnp.max(logits, axis=-1, keepdims=True)
    e = jnp.exp(logits - m)
    aff_ref[...] = e / jnp.sum(e, axis=-1, keepdims=True)


def _norm_router(x2d, w, sc, sh, wr_pad):
    tm = 256
    per_b = S // tm
    return pl.pallas_call(
        _norm_router_kernel,
        grid=(T // tm,),
        in_specs=[pl.BlockSpec((tm, D), lambda i: (i, 0)),
                  pl.BlockSpec((1, D), lambda i: (0, 0)),
                  pl.BlockSpec((1, 1, D), lambda i: (i // per_b, 0, 0)),
                  pl.BlockSpec((1, 1, D), lambda i: (i // per_b, 0, 0)),
                  pl.BlockSpec((D, LANE), lambda i: (0, 0))],
        out_specs=[pl.BlockSpec((tm, D), lambda i: (i, 0)),
                   pl.BlockSpec((tm, LANE), lambda i: (i, 0))],
        out_shape=[jax.ShapeDtypeStruct((T, D), BF16),
                   jax.ShapeDtypeStruct((T, LANE), F32)],
        compiler_params=_cp(("arbitrary",), 40),
        name="norm2_router",
    )(x2d, w, sc, sh, wr_pad)


def _topk_kernel(a_ref, u_ref, slot_ref, w_ref):
    a = a_ref[0]
    bits = lax.bitcast_convert_type(a, jnp.int32)
    t = jnp.zeros((N_EXPERTS, 1), jnp.int32)
    for bit in range(30, -1, -1):
        cand = t | (1 << bit)
        cnt = jnp.sum(jnp.where(bits >= cand, 1.0, 0.0), axis=1, keepdims=True)
        t = jnp.where(cnt >= CAP, cand, t)
    gt = bits > t
    tie = bits == t
    need = CAP - jnp.sum(jnp.where(gt, 1.0, 0.0), axis=1, keepdims=True)
    u = u_ref[...]
    tie_rank = _dot(jnp.where(tie, 1.0, 0.0).astype(BF16), u)
    sel = gt | (tie & (tie_rank < need))
    slot = _dot(jnp.where(sel, 1.0, 0.0).astype(BF16), u)
    slot_ref[0] = jnp.where(sel, slot.astype(jnp.int32), -1)
    w_ref[0] = jnp.where(sel, a, 0.0)


def _topk(aff_t, upper):
    return pl.pallas_call(
        _topk_kernel,
        grid=(B,),
        in_specs=[pl.BlockSpec((1, N_EXPERTS, S), lambda b: (b, 0, 0)),
                  pl.BlockSpec((S, S), lambda b: (0, 0))],
        out_specs=[pl.BlockSpec((1, N_EXPERTS, S), lambda b: (b, 0, 0)),
                   pl.BlockSpec((1, N_EXPERTS, S), lambda b: (b, 0, 0))],
        out_shape=[jax.ShapeDtypeStruct((B, N_EXPERTS, S), jnp.int32),
                   jax.ShapeDtypeStruct((B, N_EXPERTS, S), F32)],
        compiler_params=_cp(("arbitrary",), 40),
        name="expert_topk",
    )(aff_t, upper)


def _gather_kernel(slot_ref, h_ref, xe_ref):
    hblk = h_ref[0]
    j = lax.broadcasted_iota(jnp.int32, (CAP, S), 0)
    for e in range(N_EXPERTS):
        p = jnp.where(j == slot_ref[0, e:e + 1, :], 1.0, 0.0).astype(BF16)
        xe_ref[e, 0] = _dot(p, hblk).astype(xe_ref.dtype)


def _gather(slot, h3):
    td = 1024
    return pl.pallas_call(
        _gather_kernel,
        grid=(B, D // td),
        in_specs=[pl.BlockSpec((1, N_EXPERTS, S), lambda b, j: (b, 0, 0)),
                  pl.BlockSpec((1, S, td), lambda b, j: (b, 0, j))],
        out_specs=pl.BlockSpec((N_EXPERTS, 1, CAP, td), lambda b, j: (0, b, 0, j)),
        out_shape=jax.ShapeDtypeStruct((N_EXPERTS, B, CAP, D), BF16),
        compiler_params=_cp(("arbitrary", "arbitrary"), 48),
        name="expert_gather",
    )(slot, h3)


def _gateup_kernel(x_ref, wg_ref, wu_ref, a_ref):
    x = x_ref[0]
    g = _dot(x, wg_ref[0].astype(BF16))
    u = _dot(x, wu_ref[0].astype(BF16))
    a_ref[0] = (_silu(g) * u).astype(a_ref.dtype)


def _gateup(xe, w_gate, w_up):
    tf = 256
    m = B * CAP
    return pl.pallas_call(
        _gateup_kernel,
        grid=(N_EXPERTS, FF // tf),
        in_specs=[pl.BlockSpec((1, m, D), lambda e, f: (e, 0, 0)),
                  pl.BlockSpec((1, D, tf), lambda e, f: (e, 0, f)),
                  pl.BlockSpec((1, D, tf), lambda e, f: (e, 0, f))],
        out_specs=pl.BlockSpec((1, m, tf), lambda e, f: (e, 0, f)),
        out_shape=jax.ShapeDtypeStruct((N_EXPERTS, m, FF), BF16),
        compiler_params=_cp(("arbitrary", "arbitrary"), 48),
        name="expert_gate_up",
    )(xe, w_gate, w_up)


def _down_kernel(a_ref, w_ref, o_ref):
    o_ref[0] = _dot(a_ref[0], w_ref[0].astype(BF16)).astype(o_ref.dtype)


def _down(a, w_down):
    tn = 512
    m = B * CAP
    return pl.pallas_call(
        _down_kernel,
        grid=(N_EXPERTS, D // tn),
        in_specs=[pl.BlockSpec((1, m, FF), lambda e, j: (e, 0, 0)),
                  pl.BlockSpec((1, FF, tn), lambda e, j: (e, 0, j))],
        out_specs=pl.BlockSpec((1, m, tn), lambda e, j: (e, 0, j)),
        out_shape=jax.ShapeDtypeStruct((N_EXPERTS, m, D), BF16),
        compiler_params=_cp(("arbitrary", "arbitrary"), 40),
        name="expert_down",
    )(a, w_down)


def _scatter_kernel(slot_t_ref, w_t_ref, o_ref, y_ref, pw_ref):
    @pl.when(pl.program_id(1) == 0)
    def _():
        lane = lax.broadcasted_iota(jnp.int32, (S, CAP), 1)
        st = slot_t_ref[0]
        wt = w_t_ref[0]
        for e in range(N_EXPERTS):
            hit = lane == st[:, e:e + 1]
            pw_ref[:, e * CAP:(e + 1) * CAP] = jnp.where(hit, wt[:, e:e + 1], 0.0).astype(BF16)

    o = o_ref[:, 0].reshape(N_EXPERTS * CAP, o_ref.shape[-1])
    y_ref[0] = _dot(pw_ref[...], o)


def _scatter(slot_t, w_t, out4):
    tn = 512
    return pl.pallas_call(
        _scatter_kernel,
        grid=(B, D // tn),
        in_specs=[pl.BlockSpec((1, S, N_EXPERTS), lambda b, j: (b, 0, 0)),
                  pl.BlockSpec((1, S, N_EXPERTS), lambda b, j: (b, 0, 0)),
                  pl.BlockSpec((N_EXPERTS, 1, CAP, tn), lambda b, j: (0, b, 0, j))],
        out_specs=pl.BlockSpec((1, S, tn), lambda b, j: (b, 0, j)),
        out_shape=jax.ShapeDtypeStruct((B, S, D), F32),
        scratch_shapes=[pltpu.VMEM((S, N_EXPERTS * CAP), BF16)],
        compiler_params=_cp(("arbitrary", "arbitrary"), 48),
        name="expert_scatter",
    )(slot_t, w_t, out4)


def _final_kernel(x_ref, y_ref, g_ref, w_ref, o_ref):
    x = x_ref[...] + g_ref[0] * y_ref[...]
    ms = jnp.mean(x * x, axis=-1, keepdims=True)
    o_ref[...] = x * lax.rsqrt(ms + EPS) * w_ref[...]


def _final(x2d, y2d, gate2, w):
    tm = 256
    per_b = S // tm
    return pl.pallas_call(
        _final_kernel,
        grid=(T // tm,),
        in_specs=[pl.BlockSpec((tm, D), lambda i: (i, 0)),
                  pl.BlockSpec((tm, D), lambda i: (i, 0)),
                  pl.BlockSpec((1, 1, D), lambda i: (i // per_b, 0, 0)),
                  pl.BlockSpec((1, D), lambda i: (0, 0))],
        out_specs=pl.BlockSpec((tm, D), lambda i: (i, 0)),
        out_shape=jax.ShapeDtypeStruct((T, D), F32),
        compiler_params=_cp(("arbitrary",), 40),
        name="final_norm",
    )(x2d, y2d, gate2, w)


def _rope_tables():
    half = A_HEAD // 2
    inv_freq = ROPE_THETA ** (-jnp.arange(half, dtype=F32) / half)
    ang = jnp.arange(S, dtype=F32)[:, None] * inv_freq[None, :]
    cos = jnp.cos(ang)
    sin = jnp.sin(ang)
    cos2 = jnp.concatenate([cos, cos], axis=-1)
    sin2 = jnp.concatenate([-sin, sin], axis=-1)
    scale = A_HEAD ** -0.5
    return jnp.stack([cos2 * scale, cos2]), jnp.stack([sin2 * scale, sin2])


def kernel(x, c, norm1_w, norm2_w, normf_w, w_ada, b_ada, w_in, conv_w, conv_b, dt_bias_f, dt_bias_b,
           a_log_f, a_log_b, d_skip, ssm_norm_w, w_ssm_out, w_attn_out, w_o, w_router, w_gate_e, w_up_e,
           w_down_e):
    assert x.shape == (B, S, D) and c.shape == (B, D) and w_in.shape[0] == 1
    x2d = x.reshape(T, D)
    layer = 0

    c8 = jnp.pad(c, ((0, 8 - B), (0, 0)))
    mod = _ada(c8, w_ada[layer], b_ada[layer][None, :])[:B]
    shift1, scale1, gate1, shift2, scale2, gate2 = [m[:, None, :] for m in jnp.split(mod, 6, axis=-1)]

    h1 = _norm_mod(x2d, norm1_w[layer][None, :], scale1, shift1)

    w_in_l = w_in[layer]
    zx = _mm(h1, w_in_l, col0=0, n=OFF_DT, tm=1024, tn=512, out_dtype=BF16, vmem_mb=48, name="proj_z_xbc")
    dt, dt_t = _dt_proj(h1, w_in_l, w_in_l[:, OFF_DT:OFF_Q].T)
    w_qkvg = w_in_l[:, OFF_Q:].astype(BF16)
    cos_tab, sin_tab = _rope_tables()
    qk = _mm_rope(h1, w_qkvg, cos_tab, sin_tab, n=2 * A_WIDTH, tm=1024, tn=1024)
    vg = _mm(h1, w_qkvg, col0=2 * A_WIDTH, n=VG_W, tm=1024, tn=1024, out_dtype=BF16, vmem_mb=48,
             name="proj_v_gates")

    xbc = _conv_silu(zx.reshape(B, S, OFF_DT), conv_w[layer], conv_b[layer][None, :]).reshape(T, CONV_DIM)
    bias = jnp.concatenate([dt_bias_f[layer], dt_bias_b[layer]])
    alog = jnp.concatenate([a_log_f[layer], a_log_b[layer]])
    dsk = jnp.repeat(d_skip[layer], HEAD_P)[None, :]
    expand = (jnp.arange(D_INNER)[None, :] // HEAD_P == jnp.arange(N_HEADS)[:, None]).astype(BF16)
    yf, yb = _ssd(xbc, dt, dt_t, bias[None, :], bias[:, None], alog[None, :], alog[:, None], dsk, expand)
    yn = _gated_norm(yf, yb, zx, ssm_norm_w[layer][None, :])

    outs, lses = [], []
    for g, (window, dil) in enumerate(PATTERNS):
        length = S // dil
        hb = 4 if dil == 1 else 8
        o_g, lse_g = _attention_group(qk.reshape(B, length, dil * 2 * A_WIDTH),
                                      vg.reshape(B, length, dil * VG_W), g, dil, hb)
        outs.append(o_g.reshape(T, A_OUT))
        lses.append(lse_g.transpose(0, 3, 1, 2, 4).reshape(T, A_HPG))
    oc = _combine_groups(outs[0], outs[1], outs[2], jnp.concatenate(lses, axis=-1))

    merged = _merge(yn, oc, w_ssm_out[layer], w_attn_out[layer], vg)
    x1 = _wo_residual(merged, w_o[layer], x2d, gate1)

    wr_pad = jnp.pad(w_router[layer], ((0, 0), (0, LANE - N_EXPERTS)))
    h2, aff = _norm_router(x1, norm2_w[layer][None, :], scale2, shift2, wr_pad)
    aff_t = aff[:, :N_EXPERTS].reshape(B, S, N_EXPERTS).transpose(0, 2, 1)
    upper = (jnp.arange(S)[:, None] < jnp.arange(S)[None, :]).astype(BF16)
    slot, wsel = _topk(aff_t, upper)
    xe = _gather(slot, h2.reshape(B, S, D))
    act = _gateup(xe.reshape(N_EXPERTS, B * CAP, D), w_gate_e[layer], w_up_e[layer])
    out_e = _down(act, w_down_e[layer])
    y = _scatter(slot.transpose(0, 2, 1), wsel.transpose(0, 2, 1), out_e.reshape(N_EXPERTS, B, CAP, D))

    out = _final(x1, y.reshape(T, D), gate2, normf_w[None, :])
    return out.reshape(B, S, D)
```

```python
import functools
import math

import jax
import jax.numpy as jnp
from jax import lax
from jax.experimental import pallas as pl
from jax.experimental.pallas import tpu as pltpu

F32 = jnp.float32
BF16 = jnp.bfloat16

D = 4096
B = 4
S = 2048
T = B * S
HEAD_P = 64
D_INNER = D // 2
N_HEADS = D_INNER // HEAD_P
HEADS_PER_GROUP = 4
N_GROUPS = N_HEADS // HEADS_PER_GROUP
N_STATE = 128
CONV_K = 5
CHUNK = 128
N_CHUNKS = S // CHUNK
CONV_DIM = D_INNER + 2 * N_GROUPS * N_STATE
PATTERNS = ((128, 1), (512, 4), (2048, 16))
A_HEAD = 128
A_HPG = D // 512
A_WIDTH = len(PATTERNS) * A_HPG * A_HEAD
A_OUT = A_HPG * A_HEAD
ROPE_THETA = 10000.0
NEG_INF = -1e30
N_EXPERTS = 16
FF = D // 2
CAP = 2 * S // N_EXPERTS
EPS = 1e-6
IN_SIZES = (D_INNER, CONV_DIM, 2 * N_HEADS, A_WIDTH, A_WIDTH, A_WIDTH, 2 * D)
OFF_DT = D_INNER + CONV_DIM
OFF_Q = OFF_DT + 2 * N_HEADS
N_QKVG = 3 * A_WIDTH + 2 * D
VG_W = A_WIDTH + 2 * D
LANE = 128
REPACK_SHIFT = OFF_Q % LANE


def _cp(sem, vmem_mb=None):
    kw = dict(dimension_semantics=sem)
    if vmem_mb is not None:
        kw["vmem_limit_bytes"] = vmem_mb << 20
    return pltpu.CompilerParams(**kw)


def _dot(a, b):
    return jnp.dot(a, b, preferred_element_type=F32)


def _dot_nt(a, b):
    return lax.dot_general(a, b, (((1,), (1,)), ((), ())), preferred_element_type=F32)


def _split2(v):
    h = v.astype(BF16)
    l = (v - h.astype(F32)).astype(BF16)
    return h, l


def _split3(v):
    h = v.astype(BF16)
    r = v - h.astype(F32)
    m = r.astype(BF16)
    l = (r - m.astype(F32)).astype(BF16)
    return h, m, l


def _silu(v):
    return v * jax.nn.sigmoid(v)


def _softplus(v):
    return jnp.maximum(v, 0.0) + jnp.log1p(jnp.exp(-jnp.abs(v)))


def _ada_kernel(c_ref, w_ref, b_ref, o_ref):
    ca = _silu(c_ref[...]).astype(BF16)
    o_ref[...] = _dot(ca, w_ref[...].astype(BF16)) + b_ref[...]


def _ada(c8, w, b):
    n = w.shape[1]
    tn = 512
    return pl.pallas_call(
        _ada_kernel,
        grid=(n // tn,),
        in_specs=[pl.BlockSpec((8, D), lambda j: (0, 0)),
                  pl.BlockSpec((D, tn), lambda j: (0, j)),
                  pl.BlockSpec((1, tn), lambda j: (0, j))],
        out_specs=pl.BlockSpec((8, tn), lambda j: (0, j)),
        out_shape=jax.ShapeDtypeStruct((8, n), F32),
        compiler_params=_cp(("arbitrary",), 40),
        name="ada_mod",
    )(c8, w, b)


def _norm_mod_kernel(x_ref, w_ref, sc_ref, sh_ref, o_ref, *strided_refs, dilations):
    w = w_ref[...]
    sc = 1.0 + sc_ref[0]
    sh = sh_ref[0]

    x = x_ref[...]
    ms = jnp.mean(x * x, axis=-1, keepdims=True)
    hn = (x * lax.rsqrt(ms + EPS) * w * sc + sh).astype(o_ref.dtype)
    o_ref[...] = hn
    tm = x.shape[0]
    row = lax.broadcasted_iota(jnp.int32, (tm, tm), 0)
    col = lax.broadcasted_iota(jnp.int32, (tm, tm), 1)
    for d, s_ref in zip(dilations, strided_refs):
        per = tm // d
        src = (row & (per - 1)) * d + lax.shift_right_logical(row, per.bit_length() - 1)
        perm = jnp.where(col == src, 1.0, 0.0).astype(BF16)
        res = _dot(perm, hn).astype(o_ref.dtype)
        for r in range(d):
            s_ref[0, r] = res[r * per:(r + 1) * per]


def _norm_mod(x2d, w, sc, sh, dilations):
    tm = 256
    per_b = S // tm
    out_specs = [pl.BlockSpec((tm, D), lambda i: (i, 0))]
    out_shape = [jax.ShapeDtypeStruct((T, D), BF16)]
    for d in dilations:
        out_specs.append(pl.BlockSpec((1, d, tm // d, D), lambda i: (i // per_b, 0, i % per_b, 0)))
        out_shape.append(jax.ShapeDtypeStruct((B, d, S // d, D), BF16))
    return pl.pallas_call(
        functools.partial(_norm_mod_kernel, dilations=dilations),
        grid=(T // tm,),
        in_specs=[pl.BlockSpec((tm, D), lambda i: (i, 0)),
                  pl.BlockSpec((1, D), lambda i: (0, 0)),
                  pl.BlockSpec((1, 1, D), lambda i: (i // per_b, 0, 0)),
                  pl.BlockSpec((1, 1, D), lambda i: (i // per_b, 0, 0))],
        out_specs=out_specs,
        out_shape=out_shape,
        compiler_params=_cp(("arbitrary",), 48),
        name="norm1_mod",
    )(x2d, w, sc, sh)


def _mm_kernel(a_ref, w_ref, o_ref, *scratch, cast_w):
    if cast_w:
        (wb_ref,) = scratch

        @pl.when(pl.program_id(1) == 0)
        def _():
            wb_ref[...] = w_ref[...].astype(BF16)

        w = wb_ref[...]
    else:
        w = w_ref[...]
    o_ref[...] = _dot(a_ref[...], w).astype(o_ref.dtype)


def _mm(a, w, *, col0, n, tm, tn, out_dtype, vmem_mb, name):
    m, k = a.shape
    cast_w = w.dtype != BF16
    cb = col0 // tn
    scratch = [pltpu.VMEM((k, tn), BF16)] if cast_w else []
    return pl.pallas_call(
        functools.partial(_mm_kernel, cast_w=cast_w),
        grid=(n // tn, m // tm),
        in_specs=[pl.BlockSpec((tm, k), lambda j, i: (i, 0)),
                  pl.BlockSpec((k, tn), lambda j, i: (0, cb + j))],
        out_specs=pl.BlockSpec((tm, tn), lambda j, i: (i, j)),
        out_shape=jax.ShapeDtypeStruct((m, n), out_dtype),
        scratch_shapes=scratch,
        compiler_params=_cp(("arbitrary", "arbitrary"), vmem_mb),
        name=name,
    )(a, w)


def _repack_kernel(a_ref, b_ref, o_ref):
    n = a_ref.shape[1]
    ra = pltpu.roll(a_ref[...], n - REPACK_SHIFT, 1)
    rb = pltpu.roll(b_ref[...], REPACK_SHIFT, 1)
    o_ref[:, 0:n - LANE] = ra[:, 0:n - LANE].astype(o_ref.dtype)
    lane = lax.broadcasted_iota(jnp.int32, rb.shape, 1)
    o_ref[:, n - LANE:n] = jnp.where(lane < LANE - REPACK_SHIFT, ra[:, n - LANE:n], rb).astype(o_ref.dtype)


def _repack_weights(w_in_l):
    tk, tn = 1024, 1024
    n_tiles = N_QKVG // tn
    n_qkv = 3 * A_WIDTH // tn
    a0 = OFF_DT // tn
    n_groups = len(PATTERNS)

    def out_map(m, kk):
        return (kk, jnp.where(m < n_qkv, (m % n_groups) * 3 + m // n_groups, m))

    return pl.pallas_call(
        _repack_kernel,
        grid=(n_tiles, D // tk),
        in_specs=[pl.BlockSpec((tk, tn), lambda m, kk: (kk, a0 + m)),
                  pl.BlockSpec((tk, LANE), lambda m, kk: (kk, (OFF_DT + (m + 1) * tn) // LANE))],
        out_specs=pl.BlockSpec((tk, tn), out_map),
        out_shape=jax.ShapeDtypeStruct((D, N_QKVG), BF16),
        compiler_params=_cp(("arbitrary", "arbitrary"), 40),
        name="repack_w_qkvg",
    )(w_in_l, w_in_l)


def _qkv_kernel(a_ref, w_ref, cos_ref, sin_ref, o_ref):
    acc = _dot(a_ref[...], w_ref[...])
    j = pl.program_id(0)

    @pl.when(j < 2)
    def _():
        cos = cos_ref[0]
        sin = sin_ref[0]
        for h in range(acc.shape[1] // A_HEAD):
            t = acc[:, h * A_HEAD:(h + 1) * A_HEAD]
            rot = t * cos + pltpu.roll(t, A_HEAD // 2, 1) * sin
            o_ref[:, h * A_HEAD:(h + 1) * A_HEAD] = rot.astype(o_ref.dtype)

    @pl.when(j == 2)
    def _():
        o_ref[...] = acc.astype(o_ref.dtype)


def _qkv_group(a, w_all, cos_tab, sin_tab, g):
    tm, tn = 1024, 1024
    per_b = S // tm
    return pl.pallas_call(
        _qkv_kernel,
        grid=(3, T // tm),
        in_specs=[pl.BlockSpec((tm, D), lambda j, i: (i, 0)),
                  pl.BlockSpec((D, tn), lambda j, i: (0, 3 * g + j)),
                  pl.BlockSpec((1, tm, A_HEAD), lambda j, i: (jnp.minimum(j, 1), i % per_b, 0)),
                  pl.BlockSpec((1, tm, A_HEAD), lambda j, i: (jnp.minimum(j, 1), i % per_b, 0))],
        out_specs=pl.BlockSpec((tm, tn), lambda j, i: (i, j)),
        out_shape=jax.ShapeDtypeStruct((T, 3 * tn), BF16),
        compiler_params=_cp(("arbitrary", "arbitrary"), 48),
        name=f"proj_qkv_g{g}",
    )(a, w_all, cos_tab, sin_tab)


def _dt_kernel(a_ref, w_ref, wt_ref, o_ref, ot_ref):
    a = a_ref[...]
    o_ref[...] = _dot(a, w_ref[...].astype(BF16))
    ot_ref[...] = _dot_nt(wt_ref[...].astype(BF16), a)


def _dt_proj(a, w_in, w_dt_t):
    tm = 1024
    return pl.pallas_call(
        _dt_kernel,
        grid=(T // tm,),
        in_specs=[pl.BlockSpec((tm, D), lambda i: (i, 0)),
                  pl.BlockSpec((D, LANE), lambda i: (0, OFF_DT // LANE)),
                  pl.BlockSpec((2 * N_HEADS, D), lambda i: (0, 0))],
        out_specs=[pl.BlockSpec((tm, LANE), lambda i: (i, 0)),
                   pl.BlockSpec((2 * N_HEADS, tm), lambda i: (0, i))],
        out_shape=[jax.ShapeDtypeStruct((T, LANE), F32),
                   jax.ShapeDtypeStruct((2 * N_HEADS, T), F32)],
        compiler_params=_cp(("arbitrary",), 40),
        name="proj_dt",
    )(a, w_in, w_dt_t)


def _conv_kernel(x_ref, w_ref, b_ref, o_ref):
    x = x_ref[0].astype(F32)
    n = x.shape[0]
    row = lax.broadcasted_iota(jnp.int32, x.shape, 0)
    acc = x * w_ref[CONV_K // 2:CONV_K // 2 + 1, :] + b_ref[...]
    for k in range(CONV_K):
        off = k - CONV_K // 2
        if off == 0:
            continue
        shifted = pltpu.roll(x, (-off) % n, 0)
        ok = (row + off >= 0) & (row + off < n)
        acc = acc + jnp.where(ok, shifted, 0.0) * w_ref[k:k + 1, :]
    o_ref[0] = _silu(acc).astype(o_ref.dtype)


def _conv_silu(zx3, conv_w, conv_b):
    tc = 512
    c0 = D_INNER // tc
    return pl.pallas_call(
        _conv_kernel,
        grid=(B, CONV_DIM // tc),
        in_specs=[pl.BlockSpec((1, S, tc), lambda b, j: (b, 0, c0 + j)),
                  pl.BlockSpec((CONV_K, tc), lambda b, j: (0, j)),
                  pl.BlockSpec((1, tc), lambda b, j: (0, j))],
        out_specs=pl.BlockSpec((1, S, tc), lambda b, j: (b, 0, j)),
        out_shape=jax.ShapeDtypeStruct((B, S, CONV_DIM), BF16),
        compiler_params=_cp(("arbitrary", "arbitrary"), 48),
        name="conv_silu",
    )(zx3, conv_w, conv_b)


def _ssd_direction(x_ref, y_ref, st_ref, dt_raw, dtT_raw, bias, biasT, alog, alogT, e_ref, dsk, reverse):
    tc = CHUNK
    gw = HEADS_PER_GROUP * HEAD_P
    dt = _softplus(dt_raw + bias)
    dtT = _softplus(dtT_raw + biasT)
    da = dt * (-jnp.exp(alog))
    daT = dtT * (-jnp.exp(alogT))
    r = lax.broadcasted_iota(jnp.int32, (tc, tc), 0)
    c = lax.broadcasted_iota(jnp.int32, (tc, tc), 1)
    keep = (c >= r) if reverse else (c <= r)
    tri = jnp.where(keep, 1.0, 0.0).astype(BF16)
    triT = jnp.where((r >= c) if reverse else (r <= c), 1.0, 0.0).astype(BF16)
    h3 = _split3(da)
    cs = _dot(tri, h3[0]) + _dot(tri, h3[1]) + _dot(tri, h3[2])
    t3 = _split3(daT)
    csT = _dot(t3[0], triT) + _dot(t3[1], triT) + _dot(t3[2], triT)
    tot = cs[0:1, :] if reverse else cs[tc - 1:tc, :]
    def hi_lo(v):
        h = v.astype(BF16).astype(F32)
        return jnp.concatenate([h, v - h], axis=1).astype(BF16)

    dt2 = hi_lo(dt)
    ed2 = hi_lo(jnp.exp(tot - cs))
    ec2 = hi_lo(jnp.exp(cs))
    et2 = hi_lo(jnp.broadcast_to(jnp.exp(tot), (16, N_HEADS)))
    lanehead = lax.shift_right_logical(lax.broadcasted_iota(jnp.int32, (tc, gw), 1), HEAD_P.bit_length() - 1)
    for g in range(N_GROUPS):
        eg = e_ref[:, g * gw:(g + 1) * gw]
        dt_e = _dot(dt2, eg)
        edec_e = _dot(ed2, eg)
        ecs_e = _dot(ec2, eg)
        etot_e = _dot(et2, eg)[0:1, :]
        xs = x_ref[:, g * gw:(g + 1) * gw].astype(F32)
        bg = x_ref[:, D_INNER + g * N_STATE:D_INNER + (g + 1) * N_STATE]
        cg = x_ref[:, D_INNER + N_GROUPS * N_STATE + g * N_STATE:D_INNER + N_GROUPS * N_STATE + (g + 1) * N_STATE]
        xd = xs * dt_e
        xdd = (xd * edec_e).astype(BF16)
        cb = _dot_nt(cg, bg)
        ms = []
        rs = []
        for k in range(HEADS_PER_GROUP):
            h = HEADS_PER_GROUP * g + k
            decay = jnp.where(keep, jnp.exp(cs[:, h:h + 1] - csT[h:h + 1, :]), 0.0)
            ms.append((cb * decay).astype(BF16))
            rs.append(jnp.where(lanehead == k, xd, 0.0).astype(BF16))
        lhs = jnp.concatenate(ms, axis=1)
        rhs = jnp.concatenate(rs, axis=0)
        sg = st_ref[g]
        y = _dot(lhs, rhs) + _dot(cg, sg.astype(BF16)) * ecs_e
        if dsk is not None:
            y = y + xs * dsk[:, g * gw:(g + 1) * gw]
        y_ref[:, g * gw:(g + 1) * gw] = y
        bgT = bg.astype(F32).T.astype(BF16)
        st_ref[g] = sg * etot_e + _dot(bgT, xdd)


def _ssd_kernel(xf_ref, xb_ref, dtf_ref, dtb_ref, dtTf_ref, dtTb_ref, bias_ref, biasT_ref,
                alog_ref, alogT_ref, dsk_ref, e_ref, yf_ref, yb_ref, sf_ref, sb_ref):
    @pl.when(pl.program_id(1) == 0)
    def _():
        sf_ref[...] = jnp.zeros_like(sf_ref)
        sb_ref[...] = jnp.zeros_like(sb_ref)

    nh = N_HEADS
    _ssd_direction(xf_ref, yf_ref, sf_ref, dtf_ref[:, 0:nh], dtTf_ref[0:nh, :], bias_ref[:, 0:nh],
                   biasT_ref[0:nh, :], alog_ref[:, 0:nh], alogT_ref[0:nh, :], e_ref, dsk_ref[...], False)
    _ssd_direction(xb_ref, yb_ref, sb_ref, dtb_ref[:, nh:2 * nh], dtTb_ref[nh:2 * nh, :], bias_ref[:, nh:2 * nh],
                   biasT_ref[nh:2 * nh, :], alog_ref[:, nh:2 * nh], alogT_ref[nh:2 * nh, :], e_ref, None, True)


def _ssd(xbc, dt, dtT, bias, biasT, alog, alogT, dsk, expand):
    nc = N_CHUNKS
    fwd = lambda b, c: (b * nc + c, 0)
    bwd = lambda b, c: (b * nc + nc - 1 - c, 0)
    fwdT = lambda b, c: (0, b * nc + c)
    bwdT = lambda b, c: (0, b * nc + nc - 1 - c)
    const = lambda b, c: (0, 0)
    gw = HEADS_PER_GROUP * HEAD_P
    return pl.pallas_call(
        _ssd_kernel,
        grid=(B, nc),
        in_specs=[pl.BlockSpec((CHUNK, CONV_DIM), fwd),
                  pl.BlockSpec((CHUNK, CONV_DIM), bwd),
                  pl.BlockSpec((CHUNK, LANE), fwd),
                  pl.BlockSpec((CHUNK, LANE), bwd),
                  pl.BlockSpec((2 * N_HEADS, CHUNK), fwdT),
                  pl.BlockSpec((2 * N_HEADS, CHUNK), bwdT),
                  pl.BlockSpec((1, 2 * N_HEADS), const),
                  pl.BlockSpec((2 * N_HEADS, 1), const),
                  pl.BlockSpec((1, 2 * N_HEADS), const),
                  pl.BlockSpec((2 * N_HEADS, 1), const),
                  pl.BlockSpec((1, D_INNER), const),
                  pl.BlockSpec((2 * N_HEADS, D_INNER), const)],
        out_specs=[pl.BlockSpec((CHUNK, D_INNER), fwd),
                   pl.BlockSpec((CHUNK, D_INNER), bwd)],
        out_shape=[jax.ShapeDtypeStruct((T, D_INNER), F32),
                   jax.ShapeDtypeStruct((T, D_INNER), F32)],
        scratch_shapes=[pltpu.VMEM((N_GROUPS, N_STATE, gw), F32),
                        pltpu.VMEM((N_GROUPS, N_STATE, gw), F32)],
        compiler_params=_cp(("arbitrary", "arbitrary"), 40),
        name="ssd_scan",
    )(xbc, xbc, dt, dt, dtT, dtT, bias, biasT, alog, alogT, dsk, expand)


def _gnorm_kernel(yf_ref, yb_ref, z_ref, w_ref, o_ref):
    z = z_ref[...].astype(F32)
    y = (yf_ref[...] + yb_ref[...]) * _silu(z)
    ms = jnp.mean(y * y, axis=-1, keepdims=True)
    o_ref[...] = (y * lax.rsqrt(ms + EPS) * w_ref[...]).astype(o_ref.dtype)


def _gated_norm(yf, yb, zx, w):
    tm = 512
    return pl.pallas_call(
        _gnorm_kernel,
        grid=(T // tm,),
        in_specs=[pl.BlockSpec((tm, D_INNER), lambda i: (i, 0)),
                  pl.BlockSpec((tm, D_INNER), lambda i: (i, 0)),
                  pl.BlockSpec((tm, D_INNER), lambda i: (i, 0)),
                  pl.BlockSpec((1, D_INNER), lambda i: (0, 0))],
        out_specs=pl.BlockSpec((tm, D_INNER), lambda i: (i, 0)),
        out_shape=jax.ShapeDtypeStruct((T, D_INNER), BF16),
        compiler_params=_cp(("arbitrary",), 40),
        name="ssm_gated_norm",
    )(yf, yb, zx, w)


def _attn_kernel(q_ref, k_ref, v_ref, o_ref, lse_ref, *, length, hb, rb, radius):
    tq = 128
    win = min(length, tq + 2 * radius)
    for rr in range(rb):
        for h in range(hb):
            cols = slice(h * A_HEAD, (h + 1) * A_HEAD)
            for qb in range(length // tq):
                q0 = qb * tq
                ws = min(max(q0 - radius, 0), length - win)
                q = q_ref[0, rr, q0:q0 + tq, cols]
                k = k_ref[0, rr, ws:ws + win, cols]
                v = v_ref[0, rr, ws:ws + win, cols]
                s = _dot_nt(q, k)
                ti = q0 + lax.broadcasted_iota(jnp.int32, (tq, win), 0)
                tj = ws + lax.broadcasted_iota(jnp.int32, (tq, win), 1)
                s = jnp.where((tj - ti <= radius) & (ti - tj <= radius), s, NEG_INF)
                m = jnp.max(s, axis=-1, keepdims=True)
                p = jnp.exp(s - m)
                l = jnp.sum(p, axis=-1, keepdims=True)
                o = _dot(p.astype(BF16), v) / l
                o_ref[0, rr, q0:q0 + tq, cols] = o.astype(o_ref.dtype)
                lse_ref[0, rr, 0, q0:q0 + tq, h:h + 1] = m + jnp.log(l)


def _attention_group(qkv4, g, dil, hb, rb):
    length = S // dil
    window = PATTERNS[g][0]
    radius = window // (2 * dil)
    bw = hb * A_HEAD
    nj = A_HPG // hb
    return pl.pallas_call(
        functools.partial(_attn_kernel, length=length, hb=hb, rb=rb, radius=radius),
        grid=(B, dil // rb, nj),
        in_specs=[pl.BlockSpec((1, rb, length, bw), lambda b, r, j: (b, r, 0, j)),
                  pl.BlockSpec((1, rb, length, bw), lambda b, r, j: (b, r, 0, nj + j)),
                  pl.BlockSpec((1, rb, length, bw), lambda b, r, j: (b, r, 0, 2 * nj + j))],
        out_specs=[pl.BlockSpec((1, rb, length, bw), lambda b, r, j: (b, r, 0, j)),
                   pl.BlockSpec((1, rb, 1, length, hb), lambda b, r, j: (b, r, j, 0, 0))],
        out_shape=[jax.ShapeDtypeStruct((B, dil, length, A_OUT), BF16),
                   jax.ShapeDtypeStruct((B, dil, nj, length, hb), F32)],
        compiler_params=_cp(("arbitrary", "arbitrary", "arbitrary"), 40),
        name=f"attn_dil{dil}",
    )(qkv4, qkv4, qkv4)


def _combine_kernel(o0_ref, o1_ref, o2_ref, lse_ref, out_ref):
    lse = lse_ref[...]
    for hh in range(A_HPG):
        l0 = lse[:, hh:hh + 1]
        l1 = lse[:, A_HPG + hh:A_HPG + hh + 1]
        l2 = lse[:, 2 * A_HPG + hh:2 * A_HPG + hh + 1]
        m = jnp.maximum(jnp.maximum(l0, l1), l2)
        e0 = jnp.exp(l0 - m)
        e1 = jnp.exp(l1 - m)
        e2 = jnp.exp(l2 - m)
        inv = 1.0 / (e0 + e1 + e2)
        cols = slice(hh * A_HEAD, (hh + 1) * A_HEAD)
        mix = ((e0 * inv) * o0_ref[:, cols].astype(F32) + (e1 * inv) * o1_ref[:, cols].astype(F32)
               + (e2 * inv) * o2_ref[:, cols].astype(F32))
        out_ref[:, cols] = mix.astype(out_ref.dtype)


def _combine_groups(o0, o1, o2, lse):
    tm = 512
    n_l = lse.shape[1]
    return pl.pallas_call(
        _combine_kernel,
        grid=(T // tm,),
        in_specs=[pl.BlockSpec((tm, A_OUT), lambda i: (i, 0)),
                  pl.BlockSpec((tm, A_OUT), lambda i: (i, 0)),
                  pl.BlockSpec((tm, A_OUT), lambda i: (i, 0)),
                  pl.BlockSpec((tm, n_l), lambda i: (i, 0))],
        out_specs=pl.BlockSpec((tm, A_OUT), lambda i: (i, 0)),
        out_shape=jax.ShapeDtypeStruct((T, A_OUT), BF16),
        compiler_params=_cp(("arbitrary",), 40),
        name="attn_combine",
    )(o0, o1, o2, lse)


def _merge_kernel(ys_ref, oa_ref, ws_ref, wa_ref, gs_ref, ga_ref, o_ref, wsb_ref, wab_ref):
    @pl.when(pl.program_id(1) == 0)
    def _():
        wsb_ref[...] = ws_ref[...].astype(BF16)
        wab_ref[...] = wa_ref[...].astype(BF16)

    y_ssm = _dot(ys_ref[...], wsb_ref[...])
    y_att = _dot(oa_ref[...], wab_ref[...])
    g_s = jax.nn.sigmoid(gs_ref[...].astype(F32))
    g_a = jax.nn.sigmoid(ga_ref[...].astype(F32))
    o_ref[...] = (g_s * y_ssm + g_a * y_att).astype(o_ref.dtype)


def _merge(yn, oc, w_ssm_out, w_attn_out, gates):
    tm, tn = 1024, 512
    g0 = 0
    g1 = D // tn
    return pl.pallas_call(
        _merge_kernel,
        grid=(D // tn, T // tm),
        in_specs=[pl.BlockSpec((tm, D_INNER), lambda j, i: (i, 0)),
                  pl.BlockSpec((tm, A_OUT), lambda j, i: (i, 0)),
                  pl.BlockSpec((D_INNER, tn), lambda j, i: (0, j)),
                  pl.BlockSpec((A_OUT, tn), lambda j, i: (0, j)),
                  pl.BlockSpec((tm, tn), lambda j, i: (i, g0 + j)),
                  pl.BlockSpec((tm, tn), lambda j, i: (i, g1 + j))],
        out_specs=pl.BlockSpec((tm, tn), lambda j, i: (i, j)),
        out_shape=jax.ShapeDtypeStruct((T, D), BF16),
        scratch_shapes=[pltpu.VMEM((D_INNER, tn), BF16), pltpu.VMEM((A_OUT, tn), BF16)],
        compiler_params=_cp(("arbitrary", "arbitrary"), 48),
        name="branch_merge",
    )(yn, oc, w_ssm_out, w_attn_out, gates, gates)


def _wo_kernel(a_ref, w_ref, x_ref, g_ref, o_ref, wb_ref):
    @pl.when(pl.program_id(1) == 0)
    def _():
        wb_ref[...] = w_ref[...].astype(BF16)

    o_ref[...] = x_ref[...] + g_ref[0] * _dot(a_ref[...], wb_ref[...])


def _wo_residual(merged, w_o, x2d, gate1):
    tm, tn = 1024, 512
    per_b = S // tm
    return pl.pallas_call(
        _wo_kernel,
        grid=(D // tn, T // tm),
        in_specs=[pl.BlockSpec((tm, D), lambda j, i: (i, 0)),
                  pl.BlockSpec((D, tn), lambda j, i: (0, j)),
                  pl.BlockSpec((tm, tn), lambda j, i: (i, j)),
                  pl.BlockSpec((1, 1, tn), lambda j, i: (i // per_b, 0, j))],
        out_specs=pl.BlockSpec((tm, tn), lambda j, i: (i, j)),
        out_shape=jax.ShapeDtypeStruct((T, D), F32),
        scratch_shapes=[pltpu.VMEM((D, tn), BF16)],
        compiler_params=_cp(("arbitrary", "arbitrary"), 52),
        name="out_proj_residual",
    )(merged, w_o, x2d, gate1)


def _norm_router_kernel(x_ref, w_ref, sc_ref, sh_ref, wr_ref, h_ref, aff_ref):
    x = x_ref[...]
    ms = jnp.mean(x * x, axis=-1, keepdims=True)
    h = x * lax.rsqrt(ms + EPS) * w_ref[...] * (1.0 + sc_ref[0]) + sh_ref[0]
    h_ref[...] = h.astype(h_ref.dtype)
    hh, hl = _split2(h)
    wh, wl = _split2(wr_ref[...])
    logits = _dot(hh, wh) + _dot(hl, wh) + _dot(hh, wl)
    lane = lax.broadcasted_iota(jnp.int32, logits.shape, 1)
    logits = jnp.where(lane < N_EXPERTS, logits, NEG_INF)
    m = jnp.max(logits, axis=-1, keepdims=True)
    e = jnp.exp(logits - m)
    aff_ref[...] = e / jnp.sum(e, axis=-1, keepdims=True)


def _norm_router(x2d, w, sc, sh, wr_pad):
    tm = 256
    per_b = S // tm
    return pl.pallas_call(
        _norm_router_kernel,
        grid=(T // tm,),
        in_specs=[pl.BlockSpec((tm, D), lambda i: (i, 0)),
                  pl.BlockSpec((1, D), lambda i: (0, 0)),
                  pl.BlockSpec((1, 1, D), lambda i: (i // per_b, 0, 0)),
                  pl.BlockSpec((1, 1, D), lambda i: (i // per_b, 0, 0)),
                  pl.BlockSpec((D, LANE), lambda i: (0, 0))],
        out_specs=[pl.BlockSpec((tm, D), lambda i: (i, 0)),
                   pl.BlockSpec((tm, LANE), lambda i: (i, 0))],
        out_shape=[jax.ShapeDtypeStruct((T, D), BF16),
                   jax.ShapeDtypeStruct((T, LANE), F32)],
        compiler_params=_cp(("arbitrary",), 40),
        name="norm2_router",
    )(x2d, w, sc, sh, wr_pad)


def _topk_kernel(a_ref, u_ref, slot_ref, w_ref):
    a = a_ref[0]
    bits = lax.bitcast_convert_type(a, jnp.int32)
    t = jnp.zeros((N_EXPERTS, 1), jnp.int32)
    for bit in range(30, -1, -1):
        cand = t | (1 << bit)
        cnt = jnp.sum(jnp.where(bits >= cand, 1.0, 0.0), axis=1, keepdims=True)
        t = jnp.where(cnt >= CAP, cand, t)
    gt = bits > t
    tie = bits == t
    need = CAP - jnp.sum(jnp.where(gt, 1.0, 0.0), axis=1, keepdims=True)
    u = u_ref[...]
    tie_rank = _dot(jnp.where(tie, 1.0, 0.0).astype(BF16), u)
    sel = gt | (tie & (tie_rank < need))
    slot = _dot(jnp.where(sel, 1.0, 0.0).astype(BF16), u)
    slot_ref[0] = jnp.where(sel, slot.astype(jnp.int32), -1)
    w_ref[0] = jnp.where(sel, a, 0.0)


def _topk(aff_t, upper):
    return pl.pallas_call(
        _topk_kernel,
        grid=(B,),
        in_specs=[pl.BlockSpec((1, N_EXPERTS, S), lambda b: (b, 0, 0)),
                  pl.BlockSpec((S, S), lambda b: (0, 0))],
        out_specs=[pl.BlockSpec((1, N_EXPERTS, S), lambda b: (b, 0, 0)),
                   pl.BlockSpec((1, N_EXPERTS, S), lambda b: (b, 0, 0))],
        out_shape=[jax.ShapeDtypeStruct((B, N_EXPERTS, S), jnp.int32),
                   jax.ShapeDtypeStruct((B, N_EXPERTS, S), F32)],
        compiler_params=_cp(("arbitrary",), 40),
        name="expert_topk",
    )(aff_t, upper)


def _gather_kernel(slot_ref, h_ref, xe_ref):
    hblk = h_ref[0]
    j = lax.broadcasted_iota(jnp.int32, (CAP, S), 0)
    for e in range(N_EXPERTS):
        p = jnp.where(j == slot_ref[0, e:e + 1, :], 1.0, 0.0).astype(BF16)
        xe_ref[e, 0] = _dot(p, hblk).astype(xe_ref.dtype)


def _gather(slot, h3):
    td = 1024
    return pl.pallas_call(
        _gather_kernel,
        grid=(B, D // td),
        in_specs=[pl.BlockSpec((1, N_EXPERTS, S), lambda b, j: (b, 0, 0)),
                  pl.BlockSpec((1, S, td), lambda b, j: (b, 0, j))],
        out_specs=pl.BlockSpec((N_EXPERTS, 1, CAP, td), lambda b, j: (0, b, 0, j)),
        out_shape=jax.ShapeDtypeStruct((N_EXPERTS, B, CAP, D), BF16),
        compiler_params=_cp(("arbitrary", "arbitrary"), 48),
        name="expert_gather",
    )(slot, h3)


def _gateup_kernel(x_ref, wg_ref, wu_ref, a_ref):
    x = x_ref[0]
    g = _dot(x, wg_ref[0].astype(BF16))
    u = _dot(x, wu_ref[0].astype(BF16))
    a_ref[0] = (_silu(g) * u).astype(a_ref.dtype)


def _gateup(xe, w_gate, w_up):
    tf = 256
    m = B * CAP
    return pl.pallas_call(
        _gateup_kernel,
        grid=(N_EXPERTS, FF // tf),
        in_specs=[pl.BlockSpec((1, m, D), lambda e, f: (e, 0, 0)),
                  pl.BlockSpec((1, D, tf), lambda e, f: (e, 0, f)),
                  pl.BlockSpec((1, D, tf), lambda e, f: (e, 0, f))],
        out_specs=pl.BlockSpec((1, m, tf), lambda e, f: (e, 0, f)),
        out_shape=jax.ShapeDtypeStruct((N_EXPERTS, m, FF), BF16),
        compiler_params=_cp(("arbitrary", "arbitrary"), 48),
        name="expert_gate_up",
    )(xe, w_gate, w_up)


def _down_kernel(a_ref, w_ref, o_ref):
    o_ref[0] = _dot(a_ref[0], w_ref[0].astype(BF16)).astype(o_ref.dtype)


def _down(a, w_down):
    tn = 512
    m = B * CAP
    return pl.pallas_call(
        _down_kernel,
        grid=(N_EXPERTS, D // tn),
        in_specs=[pl.BlockSpec((1, m, FF), lambda e, j: (e, 0, 0)),
                  pl.BlockSpec((1, FF, tn), lambda e, j: (e, 0, j))],
        out_specs=pl.BlockSpec((1, m, tn), lambda e, j: (e, 0, j)),
        out_shape=jax.ShapeDtypeStruct((N_EXPERTS, m, D), BF16),
        compiler_params=_cp(("arbitrary", "arbitrary"), 40),
        name="expert_down",
    )(a, w_down)


def _scatter_kernel(slot_t_ref, w_t_ref, o_ref, y_ref, pw_ref):
    @pl.when(pl.program_id(1) == 0)
    def _():
        lane = lax.broadcasted_iota(jnp.int32, (S, CAP), 1)
        st = slot_t_ref[0]
        wt = w_t_ref[0]
        for e in range(N_EXPERTS):
            hit = lane == st[:, e:e + 1]
            pw_ref[:, e * CAP:(e + 1) * CAP] = jnp.where(hit, wt[:, e:e + 1], 0.0).astype(BF16)

    o = o_ref[:, 0].reshape(N_EXPERTS * CAP, o_ref.shape[-1])
    y_ref[0] = _dot(pw_ref[...], o)


def _scatter(slot_t, w_t, out4):
    tn = 512
    return pl.pallas_call(
        _scatter_kernel,
        grid=(B, D // tn),
        in_specs=[pl.BlockSpec((1, S, N_EXPERTS), lambda b, j: (b, 0, 0)),
                  pl.BlockSpec((1, S, N_EXPERTS), lambda b, j: (b, 0, 0)),
                  pl.BlockSpec((N_EXPERTS, 1, CAP, tn), lambda b, j: (0, b, 0, j))],
        out_specs=pl.BlockSpec((1, S, tn), lambda b, j: (b, 0, j)),
        out_shape=jax.ShapeDtypeStruct((B, S, D), F32),
        scratch_shapes=[pltpu.VMEM((S, N_EXPERTS * CAP), BF16)],
        compiler_params=_cp(("arbitrary", "arbitrary"), 48),
        name="expert_scatter",
    )(slot_t, w_t, out4)


def _final_kernel(x_ref, y_ref, g_ref, w_ref, o_ref):
    x = x_ref[...] + g_ref[0] * y_ref[...]
    ms = jnp.mean(x * x, axis=-1, keepdims=True)
    o_ref[...] = x * lax.rsqrt(ms + EPS) * w_ref[...]


def _final(x2d, y2d, gate2, w):
    tm = 256
    per_b = S // tm
    return pl.pallas_call(
        _final_kernel,
        grid=(T // tm,),
        in_specs=[pl.BlockSpec((tm, D), lambda i: (i, 0)),
                  pl.BlockSpec((tm, D), lambda i: (i, 0)),
                  pl.BlockSpec((1, 1, D), lambda i: (i // per_b, 0, 0)),
                  pl.BlockSpec((1, D), lambda i: (0, 0))],
        out_specs=pl.BlockSpec((tm, D), lambda i: (i, 0)),
        out_shape=jax.ShapeDtypeStruct((T, D), F32),
        compiler_params=_cp(("arbitrary",), 40),
        name="final_norm",
    )(x2d, y2d, gate2, w)


def _rope_tables():
    half = A_HEAD // 2
    inv_freq = ROPE_THETA ** (-jnp.arange(half, dtype=F32) / half)
    ang = jnp.arange(S, dtype=F32)[:, None] * inv_freq[None, :]
    cos = jnp.cos(ang)
    sin = jnp.sin(ang)
    cos2 = jnp.concatenate([cos, cos], axis=-1)
    sin2 = jnp.concatenate([-sin, sin], axis=-1)
    scale = A_HEAD ** -0.5
    return jnp.stack([cos2 * scale, cos2]), jnp.stack([sin2 * scale, sin2])


def kernel(x, c, norm1_w, norm2_w, normf_w, w_ada, b_ada, w_in, conv_w, conv_b, dt_bias_f, dt_bias_b,
           a_log_f, a_log_b, d_skip, ssm_norm_w, w_ssm_out, w_attn_out, w_o, w_router, w_gate_e, w_up_e,
           w_down_e):
    assert x.shape == (B, S, D) and c.shape == (B, D) and w_in.shape[0] == 1
    x2d = x.reshape(T, D)
    layer = 0

    c8 = jnp.pad(c, ((0, 8 - B), (0, 0)))
    mod = _ada(c8, w_ada[layer], b_ada[layer][None, :])[:B]
    shift1, scale1, gate1, shift2, scale2, gate2 = [m[:, None, :] for m in jnp.split(mod, 6, axis=-1)]

    dilations = tuple(d for _, d in PATTERNS)
    h1_all = _norm_mod(x2d, norm1_w[layer][None, :], scale1, shift1, dilations[1:])
    h1 = h1_all[0]

    w_in_l = w_in[layer]
    zx = _mm(h1, w_in_l, col0=0, n=OFF_DT, tm=1024, tn=512, out_dtype=BF16, vmem_mb=48, name="proj_z_xbc")
    dt, dt_t = _dt_proj(h1, w_in_l, w_in_l[:, OFF_DT:OFF_Q].T)
    w_qkvg = _repack_weights(w_in_l)
    cos_tab, sin_tab = _rope_tables()
    gates = _mm(h1, w_qkvg, col0=3 * A_WIDTH, n=2 * D, tm=1024, tn=1024, out_dtype=BF16, vmem_mb=48,
                name="proj_gates")

    xbc = _conv_silu(zx.reshape(B, S, OFF_DT), conv_w[layer], conv_b[layer][None, :]).reshape(T, CONV_DIM)
    bias = jnp.concatenate([dt_bias_f[layer], dt_bias_b[layer]])
    alog = jnp.concatenate([a_log_f[layer], a_log_b[layer]])
    dsk = jnp.repeat(d_skip[layer], HEAD_P)[None, :]
    expand = (jnp.arange(D_INNER)[None, :] // HEAD_P == jnp.arange(N_HEADS)[:, None]).astype(BF16)
    expand = jnp.concatenate([expand, expand], axis=0)
    yf, yb = _ssd(xbc, dt, dt_t, bias[None, :], bias[:, None], alog[None, :], alog[:, None], dsk, expand)
    yn = _gated_norm(yf, yb, zx, ssm_norm_w[layer][None, :])

    outs, lses = [], []
    for g, dil in enumerate(dilations):
        length = S // dil
        hb, rb = {1: (4, 1), 4: (8, 2), 16: (8, 8)}[dil]
        to_group_order = lambda t: t.reshape(2, length, dil, A_HEAD).transpose(0, 2, 1, 3).reshape(2, S, A_HEAD)
        qkv_g = _qkv_group(h1_all[g].reshape(T, D), w_qkvg, to_group_order(cos_tab), to_group_order(sin_tab), g)
        o_g, lse_g = _attention_group(qkv_g.reshape(B, dil, length, 3 * A_OUT), g, dil, hb, rb)
        outs.append(o_g.transpose(0, 2, 1, 3).reshape(T, A_OUT))
        lses.append(lse_g.transpose(0, 3, 1, 2, 4).reshape(T, A_HPG))
    oc = _combine_groups(outs[0], outs[1], outs[2], jnp.concatenate(lses, axis=-1))

    merged = _merge(yn, oc, w_ssm_out[layer], w_attn_out[layer], gates)
    x1 = _wo_residual(merged, w_o[layer], x2d, gate1)

    wr_pad = jnp.pad(w_router[layer], ((0, 0), (0, LANE - N_EXPERTS)))
    h2, aff = _norm_router(x1, norm2_w[layer][None, :], scale2, shift2, wr_pad)
    aff_t = aff[:, :N_EXPERTS].reshape(B, S, N_EXPERTS).transpose(0, 2, 1)
    upper = (jnp.arange(S)[:, None] < jnp.arange(S)[None, :]).astype(BF16)
    slot, wsel = _topk(aff_t, upper)
    xe = _gather(slot, h2.reshape(B, S, D))
    act = _gateup(xe.reshape(N_EXPERTS, B * CAP, D), w_gate_e[layer], w_up_e[layer])
    out_e = _down(act, w_down_e[layer])
    y = _scatter(slot.transpose(0, 2, 1), wsel.transpose(0, 2, 1), out_e.reshape(N_EXPERTS, B, CAP, D))

    out = _final(x1, y.reshape(T, D), gate2, normf_w[None, :])
    return out.reshape(B, S, D)
```

```python
import functools
import math

import jax
import jax.numpy as jnp
from jax import lax
from jax.experimental import pallas as pl
from jax.experimental.pallas import tpu as pltpu

F32 = jnp.float32
BF16 = jnp.bfloat16

D = 4096
B = 4
S = 2048
T = B * S
HEAD_P = 64
D_INNER = D // 2
N_HEADS = D_INNER // HEAD_P
HEADS_PER_GROUP = 4
N_GROUPS = N_HEADS // HEADS_PER_GROUP
N_STATE = 128
CONV_K = 5
CHUNK = 128
N_CHUNKS = S // CHUNK
CONV_DIM = D_INNER + 2 * N_GROUPS * N_STATE
PATTERNS = ((128, 1), (512, 4), (2048, 16))
A_HEAD = 128
A_HPG = D // 512
A_WIDTH = len(PATTERNS) * A_HPG * A_HEAD
A_OUT = A_HPG * A_HEAD
ROPE_THETA = 10000.0
NEG_INF = -1e30
N_EXPERTS = 16
FF = D // 2
CAP = 2 * S // N_EXPERTS
EPS = 1e-6
IN_SIZES = (D_INNER, CONV_DIM, 2 * N_HEADS, A_WIDTH, A_WIDTH, A_WIDTH, 2 * D)
OFF_DT = D_INNER + CONV_DIM
OFF_Q = OFF_DT + 2 * N_HEADS
N_QKVG = 3 * A_WIDTH + 2 * D
VG_W = A_WIDTH + 2 * D
LANE = 128


def _cp(sem, vmem_mb=None):
    kw = dict(dimension_semantics=sem)
    if vmem_mb is not None:
        kw["vmem_limit_bytes"] = vmem_mb << 20
    return pltpu.CompilerParams(**kw)


def _dot(a, b):
    return jnp.dot(a, b, preferred_element_type=F32)


def _dot_nt(a, b):
    return lax.dot_general(a, b, (((1,), (1,)), ((), ())), preferred_element_type=F32)


def _split2(v):
    h = v.astype(BF16)
    l = (v - h.astype(F32)).astype(BF16)
    return h, l


def _split3(v):
    h = v.astype(BF16)
    r = v - h.astype(F32)
    m = r.astype(BF16)
    l = (r - m.astype(F32)).astype(BF16)
    return h, m, l


def _silu(v):
    return v * jax.nn.sigmoid(v)


def _softplus(v):
    return jnp.maximum(v, 0.0) + jnp.log1p(jnp.exp(-jnp.abs(v)))


def _ada_kernel(c_ref, w_ref, b_ref, o_ref):
    ca = _silu(c_ref[...]).astype(BF16)
    o_ref[...] = _dot(ca, w_ref[...].astype(BF16)) + b_ref[...]


def _ada(c8, w, b):
    n = w.shape[1]
    tn = 512
    return pl.pallas_call(
        _ada_kernel,
        grid=(n // tn,),
        in_specs=[pl.BlockSpec((8, D), lambda j: (0, 0)),
                  pl.BlockSpec((D, tn), lambda j: (0, j)),
                  pl.BlockSpec((1, tn), lambda j: (0, j))],
        out_specs=pl.BlockSpec((8, tn), lambda j: (0, j)),
        out_shape=jax.ShapeDtypeStruct((8, n), F32),
        compiler_params=_cp(("arbitrary",), 40),
        name="ada_mod",
    )(c8, w, b)


def _norm_mod_kernel(x_ref, w_ref, sc_ref, sh_ref, o_ref, *strided_refs, dilations):
    w = w_ref[...]
    sc = 1.0 + sc_ref[0]
    sh = sh_ref[0]

    x = x_ref[...]
    ms = jnp.mean(x * x, axis=-1, keepdims=True)
    hn = (x * lax.rsqrt(ms + EPS) * w * sc + sh).astype(o_ref.dtype)
    o_ref[...] = hn
    tm = x.shape[0]
    row = lax.broadcasted_iota(jnp.int32, (tm, tm), 0)
    col = lax.broadcasted_iota(jnp.int32, (tm, tm), 1)
    for d, s_ref in zip(dilations, strided_refs):
        per = tm // d
        src = (row & (per - 1)) * d + lax.shift_right_logical(row, per.bit_length() - 1)
        perm = jnp.where(col == src, 1.0, 0.0).astype(BF16)
        res = _dot(perm, hn).astype(o_ref.dtype)
        for r in range(d):
            s_ref[0, r] = res[r * per:(r + 1) * per]


def _norm_mod(x2d, w, sc, sh, dilations):
    tm = 256
    per_b = S // tm
    out_specs = [pl.BlockSpec((tm, D), lambda i: (i, 0))]
    out_shape = [jax.ShapeDtypeStruct((T, D), BF16)]
    for d in dilations:
        out_specs.append(pl.BlockSpec((1, d, tm // d, D), lambda i: (i // per_b, 0, i % per_b, 0)))
        out_shape.append(jax.ShapeDtypeStruct((B, d, S // d, D), BF16))
    return pl.pallas_call(
        functools.partial(_norm_mod_kernel, dilations=dilations),
        grid=(T // tm,),
        in_specs=[pl.BlockSpec((tm, D), lambda i: (i, 0)),
                  pl.BlockSpec((1, D), lambda i: (0, 0)),
                  pl.BlockSpec((1, 1, D), lambda i: (i // per_b, 0, 0)),
                  pl.BlockSpec((1, 1, D), lambda i: (i // per_b, 0, 0))],
        out_specs=out_specs,
        out_shape=out_shape,
        compiler_params=_cp(("arbitrary",), 48),
        name="norm1_mod",
    )(x2d, w, sc, sh)


def _mm_kernel(a_ref, w_ref, o_ref, *scratch, cast_w):
    if cast_w:
        (wb_ref,) = scratch

        @pl.when(pl.program_id(1) == 0)
        def _():
            wb_ref[...] = w_ref[...].T.astype(BF16)

        w = wb_ref[...]
    else:
        w = w_ref[...]
    o_ref[...] = _dot(a_ref[...], w).astype(o_ref.dtype)


def _mm(a, w, *, col0, n, tm, tn, out_dtype, vmem_mb, name):
    m, k = a.shape
    cast_w = w.dtype != BF16
    cb = col0 // tn
    scratch = [pltpu.VMEM((k, tn), BF16)] if cast_w else []
    if cast_w:
        w_spec = pl.BlockSpec((tn, k), lambda j, i: (cb + j, 0))
    else:
        w_spec = pl.BlockSpec((k, tn), lambda j, i: (0, cb + j))
    return pl.pallas_call(
        functools.partial(_mm_kernel, cast_w=cast_w),
        grid=(n // tn, m // tm),
        in_specs=[pl.BlockSpec((tm, k), lambda j, i: (i, 0)),
                  w_spec],
        out_specs=pl.BlockSpec((tm, tn), lambda j, i: (i, j)),
        out_shape=jax.ShapeDtypeStruct((m, n), out_dtype),
        scratch_shapes=scratch,
        compiler_params=_cp(("arbitrary", "arbitrary"), vmem_mb),
        name=name,
    )(a, w)


def _repack_kernel(a_ref, o_ref):
    o_ref[...] = a_ref[...].T.astype(o_ref.dtype)


def _repack_weights(w_t):
    tk, tn = 1024, 1024
    n_tiles = N_QKVG // tn
    n_qkv = 3 * A_WIDTH // tn
    n_groups = len(PATTERNS)

    def out_map(m, kk):
        return (kk, jnp.where(m < n_qkv, (m % n_groups) * 3 + m // n_groups, m))

    return pl.pallas_call(
        _repack_kernel,
        grid=(n_tiles, D // tk),
        in_specs=[pl.BlockSpec((pl.Element(tn), pl.Element(tk)),
                               lambda m, kk: (pl.multiple_of(OFF_Q + m * tn, math.gcd(OFF_Q, tn)),
                                              pl.multiple_of(kk * tk, tk)))],
        out_specs=pl.BlockSpec((tk, tn), out_map),
        out_shape=jax.ShapeDtypeStruct((D, N_QKVG), BF16),
        compiler_params=_cp(("arbitrary", "arbitrary"), 40),
        name="repack_w_qkvg",
    )(w_t)


def _qkv_kernel(a_ref, w_ref, cos_ref, sin_ref, o_ref):
    acc = _dot(a_ref[...], w_ref[...])
    j = pl.program_id(0)

    @pl.when(j < 2)
    def _():
        cos = cos_ref[0]
        sin = sin_ref[0]
        for h in range(acc.shape[1] // A_HEAD):
            t = acc[:, h * A_HEAD:(h + 1) * A_HEAD]
            rot = t * cos + pltpu.roll(t, A_HEAD // 2, 1) * sin
            o_ref[:, h * A_HEAD:(h + 1) * A_HEAD] = rot.astype(o_ref.dtype)

    @pl.when(j == 2)
    def _():
        o_ref[...] = acc.astype(o_ref.dtype)


def _qkv_group(a, w_all, cos_tab, sin_tab, g):
    tm, tn = 1024, 1024
    per_b = S // tm
    return pl.pallas_call(
        _qkv_kernel,
        grid=(3, T // tm),
        in_specs=[pl.BlockSpec((tm, D), lambda j, i: (i, 0)),
                  pl.BlockSpec((D, tn), lambda j, i: (0, 3 * g + j)),
                  pl.BlockSpec((1, tm, A_HEAD), lambda j, i: (jnp.minimum(j, 1), i % per_b, 0)),
                  pl.BlockSpec((1, tm, A_HEAD), lambda j, i: (jnp.minimum(j, 1), i % per_b, 0))],
        out_specs=pl.BlockSpec((tm, tn), lambda j, i: (i, j)),
        out_shape=jax.ShapeDtypeStruct((T, 3 * tn), BF16),
        compiler_params=_cp(("arbitrary", "arbitrary"), 48),
        name=f"proj_qkv_g{g}",
    )(a, w_all, cos_tab, sin_tab)


def _dt_kernel(a_ref, w_ref, o_ref, ot_ref):
    ot = _dot_nt(w_ref[...].astype(BF16), a_ref[...])
    ot_ref[...] = ot[0:2 * N_HEADS, :]
    o_ref[...] = ot.T


def _dt_proj(a, w_t):
    tm = 1024
    return pl.pallas_call(
        _dt_kernel,
        grid=(T // tm,),
        in_specs=[pl.BlockSpec((tm, D), lambda i: (i, 0)),
                  pl.BlockSpec((LANE, D), lambda i: (OFF_DT // LANE, 0))],
        out_specs=[pl.BlockSpec((tm, LANE), lambda i: (i, 0)),
                   pl.BlockSpec((2 * N_HEADS, tm), lambda i: (0, i))],
        out_shape=[jax.ShapeDtypeStruct((T, LANE), F32),
                   jax.ShapeDtypeStruct((2 * N_HEADS, T), F32)],
        compiler_params=_cp(("arbitrary",), 40),
        name="proj_dt",
    )(a, w_t)


def _conv_kernel(x_ref, w_ref, b_ref, o_ref):
    x = x_ref[0].astype(F32)
    n = x.shape[0]
    row = lax.broadcasted_iota(jnp.int32, x.shape, 0)
    acc = x * w_ref[CONV_K // 2:CONV_K // 2 + 1, :] + b_ref[...]
    for k in range(CONV_K):
        off = k - CONV_K // 2
        if off == 0:
            continue
        shifted = pltpu.roll(x, (-off) % n, 0)
        ok = (row + off >= 0) & (row + off < n)
        acc = acc + jnp.where(ok, shifted, 0.0) * w_ref[k:k + 1, :]
    o_ref[0] = _silu(acc).astype(o_ref.dtype)


def _conv_silu(zx3, conv_w, conv_b):
    tc = 512
    c0 = D_INNER // tc
    return pl.pallas_call(
        _conv_kernel,
        grid=(B, CONV_DIM // tc),
        in_specs=[pl.BlockSpec((1, S, tc), lambda b, j: (b, 0, c0 + j)),
                  pl.BlockSpec((CONV_K, tc), lambda b, j: (0, j)),
                  pl.BlockSpec((1, tc), lambda b, j: (0, j))],
        out_specs=pl.BlockSpec((1, S, tc), lambda b, j: (b, 0, j)),
        out_shape=jax.ShapeDtypeStruct((B, S, CONV_DIM), BF16),
        compiler_params=_cp(("arbitrary", "arbitrary"), 48),
        name="conv_silu",
    )(zx3, conv_w, conv_b)


def _ssd_direction(x_ref, y_ref, st_ref, dt_raw, dtT_raw, bias, biasT, alog, alogT, e_ref, dsk, reverse):
    tc = CHUNK
    gw = HEADS_PER_GROUP * HEAD_P
    dt = _softplus(dt_raw + bias)
    dtT = _softplus(dtT_raw + biasT)
    da = dt * (-jnp.exp(alog))
    daT = dtT * (-jnp.exp(alogT))
    r = lax.broadcasted_iota(jnp.int32, (tc, tc), 0)
    c = lax.broadcasted_iota(jnp.int32, (tc, tc), 1)
    keep = (c >= r) if reverse else (c <= r)
    tri = jnp.where(keep, 1.0, 0.0).astype(BF16)
    triT = jnp.where((r >= c) if reverse else (r <= c), 1.0, 0.0).astype(BF16)
    h3 = _split3(da)
    cs = _dot(tri, h3[0]) + _dot(tri, h3[1]) + _dot(tri, h3[2])
    t3 = _split3(daT)
    csT = _dot(t3[0], triT) + _dot(t3[1], triT) + _dot(t3[2], triT)
    tot = cs[0:1, :] if reverse else cs[tc - 1:tc, :]
    def hi_lo(v):
        h = v.astype(BF16).astype(F32)
        return jnp.concatenate([h, v - h], axis=1).astype(BF16)

    dt2 = hi_lo(dt)
    ed2 = hi_lo(jnp.exp(tot - cs))
    ec2 = hi_lo(jnp.exp(cs))
    et2 = hi_lo(jnp.broadcast_to(jnp.exp(tot), (16, N_HEADS)))
    lanehead = lax.shift_right_logical(lax.broadcasted_iota(jnp.int32, (tc, gw), 1), HEAD_P.bit_length() - 1)
    for g in range(N_GROUPS):
        eg = e_ref[:, g * gw:(g + 1) * gw]
        dt_e = _dot(dt2, eg)
        edec_e = _dot(ed2, eg)
        ecs_e = _dot(ec2, eg)
        etot_e = _dot(et2, eg)[0:1, :]
        xs = x_ref[:, g * gw:(g + 1) * gw].astype(F32)
        bg = x_ref[:, D_INNER + g * N_STATE:D_INNER + (g + 1) * N_STATE]
        cg = x_ref[:, D_INNER + N_GROUPS * N_STATE + g * N_STATE:D_INNER + N_GROUPS * N_STATE + (g + 1) * N_STATE]
        xd = xs * dt_e
        xdd = (xd * edec_e).astype(BF16)
        cb = _dot_nt(cg, bg)
        ms = []
        rs = []
        for k in range(HEADS_PER_GROUP):
            h = HEADS_PER_GROUP * g + k
            decay = jnp.where(keep, jnp.exp(cs[:, h:h + 1] - csT[h:h + 1, :]), 0.0)
            ms.append((cb * decay).astype(BF16))
            rs.append(jnp.where(lanehead == k, xd, 0.0).astype(BF16))
        lhs = jnp.concatenate(ms, axis=1)
        rhs = jnp.concatenate(rs, axis=0)
        sg = st_ref[g]
        y = _dot(lhs, rhs) + _dot(cg, sg.astype(BF16)) * ecs_e
        if dsk is not None:
            y = y + xs * dsk[:, g * gw:(g + 1) * gw]
        y_ref[:, g * gw:(g + 1) * gw] = y.astype(y_ref.dtype)
        bgT = bg.astype(F32).T.astype(BF16)
        st_ref[g] = sg * etot_e + _dot(bgT, xdd)


def _ssd_kernel(xf_ref, xb_ref, dtf_ref, dtb_ref, dtTf_ref, dtTb_ref, bias_ref, biasT_ref,
                alog_ref, alogT_ref, dsk_ref, e_ref, yf_ref, yb_ref, sf_ref, sb_ref):
    @pl.when(pl.program_id(1) == 0)
    def _():
        sf_ref[...] = jnp.zeros_like(sf_ref)
        sb_ref[...] = jnp.zeros_like(sb_ref)

    nh = N_HEADS
    _ssd_direction(xf_ref, yf_ref, sf_ref, dtf_ref[:, 0:nh], dtTf_ref[0:nh, :], bias_ref[:, 0:nh],
                   biasT_ref[0:nh, :], alog_ref[:, 0:nh], alogT_ref[0:nh, :], e_ref, dsk_ref[...], False)
    _ssd_direction(xb_ref, yb_ref, sb_ref, dtb_ref[:, nh:2 * nh], dtTb_ref[nh:2 * nh, :], bias_ref[:, nh:2 * nh],
                   biasT_ref[nh:2 * nh, :], alog_ref[:, nh:2 * nh], alogT_ref[nh:2 * nh, :], e_ref, None, True)


def _ssd(xbc, dt, dtT, bias, biasT, alog, alogT, dsk, expand):
    nc = N_CHUNKS
    fwd = lambda b, c: (b * nc + c, 0)
    bwd = lambda b, c: (b * nc + nc - 1 - c, 0)
    fwdT = lambda b, c: (0, b * nc + c)
    bwdT = lambda b, c: (0, b * nc + nc - 1 - c)
    const = lambda b, c: (0, 0)
    gw = HEADS_PER_GROUP * HEAD_P
    return pl.pallas_call(
        _ssd_kernel,
        grid=(B, nc),
        in_specs=[pl.BlockSpec((CHUNK, CONV_DIM), fwd),
                  pl.BlockSpec((CHUNK, CONV_DIM), bwd),
                  pl.BlockSpec((CHUNK, LANE), fwd),
                  pl.BlockSpec((CHUNK, LANE), bwd),
                  pl.BlockSpec((2 * N_HEADS, CHUNK), fwdT),
                  pl.BlockSpec((2 * N_HEADS, CHUNK), bwdT),
                  pl.BlockSpec((1, 2 * N_HEADS), const),
                  pl.BlockSpec((2 * N_HEADS, 1), const),
                  pl.BlockSpec((1, 2 * N_HEADS), const),
                  pl.BlockSpec((2 * N_HEADS, 1), const),
                  pl.BlockSpec((1, D_INNER), const),
                  pl.BlockSpec((2 * N_HEADS, D_INNER), const)],
        out_specs=[pl.BlockSpec((CHUNK, D_INNER), fwd),
                   pl.BlockSpec((CHUNK, D_INNER), bwd)],
        out_shape=[jax.ShapeDtypeStruct((T, D_INNER), BF16),
                   jax.ShapeDtypeStruct((T, D_INNER), BF16)],
        scratch_shapes=[pltpu.VMEM((N_GROUPS, N_STATE, gw), F32),
                        pltpu.VMEM((N_GROUPS, N_STATE, gw), F32)],
        compiler_params=_cp(("arbitrary", "arbitrary"), 40),
        name="ssd_scan",
    )(xbc, xbc, dt, dt, dtT, dtT, bias, biasT, alog, alogT, dsk, expand)


def _gnorm_kernel(yf_ref, yb_ref, z_ref, w_ref, o_ref):
    z = z_ref[...].astype(F32)
    y = (yf_ref[...].astype(F32) + yb_ref[...].astype(F32)) * _silu(z)
    ms = jnp.mean(y * y, axis=-1, keepdims=True)
    o_ref[...] = (y * lax.rsqrt(ms + EPS) * w_ref[...]).astype(o_ref.dtype)


def _gated_norm(yf, yb, zx, w):
    tm = 512
    return pl.pallas_call(
        _gnorm_kernel,
        grid=(T // tm,),
        in_specs=[pl.BlockSpec((tm, D_INNER), lambda i: (i, 0)),
                  pl.BlockSpec((tm, D_INNER), lambda i: (i, 0)),
                  pl.BlockSpec((tm, D_INNER), lambda i: (i, 0)),
                  pl.BlockSpec((1, D_INNER), lambda i: (0, 0))],
        out_specs=pl.BlockSpec((tm, D_INNER), lambda i: (i, 0)),
        out_shape=jax.ShapeDtypeStruct((T, D_INNER), BF16),
        compiler_params=_cp(("arbitrary",), 40),
        name="ssm_gated_norm",
    )(yf, yb, zx, w)


def _attn_kernel(q_ref, k_ref, v_ref, o_ref, lse_ref, *, length, hb, rb, radius):
    tq = 128
    win = min(length, tq + 2 * radius)
    for rr in range(rb):
        for h in range(hb):
            cols = slice(h * A_HEAD, (h + 1) * A_HEAD)
            for qb in range(length // tq):
                q0 = qb * tq
                ws = min(max(q0 - radius, 0), length - win)
                q = q_ref[0, rr, q0:q0 + tq, cols]
                k = k_ref[0, rr, ws:ws + win, cols]
                v = v_ref[0, rr, ws:ws + win, cols]
                s = _dot_nt(q, k)
                ti = q0 + lax.broadcasted_iota(jnp.int32, (tq, win), 0)
                tj = ws + lax.broadcasted_iota(jnp.int32, (tq, win), 1)
                s = jnp.where((tj - ti <= radius) & (ti - tj <= radius), s, NEG_INF)
                m = jnp.max(s, axis=-1, keepdims=True)
                p = jnp.exp(s - m)
                l = jnp.sum(p, axis=-1, keepdims=True)
                o = _dot(p.astype(BF16), v) / l
                o_ref[0, rr, q0:q0 + tq, cols] = o.astype(o_ref.dtype)
                lse_ref[0, rr, 0, q0:q0 + tq, h:h + 1] = m + jnp.log(l)


def _attention_group(qkv4, g, dil, hb, rb):
    length = S // dil
    window = PATTERNS[g][0]
    radius = window // (2 * dil)
    bw = hb * A_HEAD
    nj = A_HPG // hb
    return pl.pallas_call(
        functools.partial(_attn_kernel, length=length, hb=hb, rb=rb, radius=radius),
        grid=(B, dil // rb, nj),
        in_specs=[pl.BlockSpec((1, rb, length, bw), lambda b, r, j: (b, r, 0, j)),
                  pl.BlockSpec((1, rb, length, bw), lambda b, r, j: (b, r, 0, nj + j)),
                  pl.BlockSpec((1, rb, length, bw), lambda b, r, j: (b, r, 0, 2 * nj + j))],
        out_specs=[pl.BlockSpec((1, rb, length, bw), lambda b, r, j: (b, r, 0, j)),
                   pl.BlockSpec((1, rb, 1, length, hb), lambda b, r, j: (b, r, j, 0, 0))],
        out_shape=[jax.ShapeDtypeStruct((B, dil, length, A_OUT), BF16),
                   jax.ShapeDtypeStruct((B, dil, nj, length, hb), F32)],
        compiler_params=_cp(("arbitrary", "arbitrary", "arbitrary"), 40),
        name=f"attn_dil{dil}",
    )(qkv4, qkv4, qkv4)


def _combine_kernel(o0_ref, o1_ref, o2_ref, lse_ref, out_ref):
    lse = lse_ref[...]
    for hh in range(A_HPG):
        l0 = lse[:, hh:hh + 1]
        l1 = lse[:, A_HPG + hh:A_HPG + hh + 1]
        l2 = lse[:, 2 * A_HPG + hh:2 * A_HPG + hh + 1]
        m = jnp.maximum(jnp.maximum(l0, l1), l2)
        e0 = jnp.exp(l0 - m)
        e1 = jnp.exp(l1 - m)
        e2 = jnp.exp(l2 - m)
        inv = 1.0 / (e0 + e1 + e2)
        cols = slice(hh * A_HEAD, (hh + 1) * A_HEAD)
        mix = ((e0 * inv) * o0_ref[:, cols].astype(F32) + (e1 * inv) * o1_ref[:, cols].astype(F32)
               + (e2 * inv) * o2_ref[:, cols].astype(F32))
        out_ref[:, cols] = mix.astype(out_ref.dtype)


def _combine_groups(o0, o1, o2, lse):
    tm = 512
    n_l = lse.shape[1]
    return pl.pallas_call(
        _combine_kernel,
        grid=(T // tm,),
        in_specs=[pl.BlockSpec((tm, A_OUT), lambda i: (i, 0)),
                  pl.BlockSpec((tm, A_OUT), lambda i: (i, 0)),
                  pl.BlockSpec((tm, A_OUT), lambda i: (i, 0)),
                  pl.BlockSpec((tm, n_l), lambda i: (i, 0))],
        out_specs=pl.BlockSpec((tm, A_OUT), lambda i: (i, 0)),
        out_shape=jax.ShapeDtypeStruct((T, A_OUT), BF16),
        compiler_params=_cp(("arbitrary",), 40),
        name="attn_combine",
    )(o0, o1, o2, lse)


def _merge_kernel(ys_ref, oa_ref, ws_ref, wa_ref, gs_ref, ga_ref, o_ref, wsb_ref, wab_ref):
    @pl.when(pl.program_id(1) == 0)
    def _():
        wsb_ref[...] = ws_ref[...].astype(BF16)
        wab_ref[...] = wa_ref[...].astype(BF16)

    y_ssm = _dot(ys_ref[...], wsb_ref[...])
    y_att = _dot(oa_ref[...], wab_ref[...])
    g_s = jax.nn.sigmoid(gs_ref[...].astype(F32))
    g_a = jax.nn.sigmoid(ga_ref[...].astype(F32))
    o_ref[...] = (g_s * y_ssm + g_a * y_att).astype(o_ref.dtype)


def _merge(yn, oc, w_ssm_out, w_attn_out, gates):
    tm, tn = 1024, 512
    g0 = 0
    g1 = D // tn
    return pl.pallas_call(
        _merge_kernel,
        grid=(D // tn, T // tm),
        in_specs=[pl.BlockSpec((tm, D_INNER), lambda j, i: (i, 0)),
                  pl.BlockSpec((tm, A_OUT), lambda j, i: (i, 0)),
                  pl.BlockSpec((D_INNER, tn), lambda j, i: (0, j)),
                  pl.BlockSpec((A_OUT, tn), lambda j, i: (0, j)),
                  pl.BlockSpec((tm, tn), lambda j, i: (i, g0 + j)),
                  pl.BlockSpec((tm, tn), lambda j, i: (i, g1 + j))],
        out_specs=pl.BlockSpec((tm, tn), lambda j, i: (i, j)),
        out_shape=jax.ShapeDtypeStruct((T, D), BF16),
        scratch_shapes=[pltpu.VMEM((D_INNER, tn), BF16), pltpu.VMEM((A_OUT, tn), BF16)],
        compiler_params=_cp(("arbitrary", "arbitrary"), 48),
        name="branch_merge",
    )(yn, oc, w_ssm_out, w_attn_out, gates, gates)


def _wo_kernel(a_ref, w_ref, x_ref, g_ref, o_ref, wb_ref):
    @pl.when(pl.program_id(1) == 0)
    def _():
        wb_ref[...] = w_ref[...].astype(BF16)

    o_ref[...] = x_ref[...] + g_ref[0] * _dot(a_ref[...], wb_ref[...])


def _wo_residual(merged, w_o, x2d, gate1):
    tm, tn = 1024, 512
    per_b = S // tm
    return pl.pallas_call(
        _wo_kernel,
        grid=(D // tn, T // tm),
        in_specs=[pl.BlockSpec((tm, D), lambda j, i: (i, 0)),
                  pl.BlockSpec((D, tn), lambda j, i: (0, j)),
                  pl.BlockSpec((tm, tn), lambda j, i: (i, j)),
                  pl.BlockSpec((1, 1, tn), lambda j, i: (i // per_b, 0, j))],
        out_specs=pl.BlockSpec((tm, tn), lambda j, i: (i, j)),
        out_shape=jax.ShapeDtypeStruct((T, D), F32),
        scratch_shapes=[pltpu.VMEM((D, tn), BF16)],
        compiler_params=_cp(("arbitrary", "arbitrary"), 52),
        name="out_proj_residual",
    )(merged, w_o, x2d, gate1)


def _norm_router_kernel(x_ref, w_ref, sc_ref, sh_ref, wr_ref, h_ref, aff_ref):
    x = x_ref[...]
    ms = jnp.mean(x * x, axis=-1, keepdims=True)
    h = x * lax.rsqrt(ms + EPS) * w_ref[...] * (1.0 + sc_ref[0]) + sh_ref[0]
    h_ref[...] = h.astype(h_ref.dtype)
    hh, hl = _split2(h)
    wh, wl = _split2(wr_ref[...])
    logits = _dot(hh, wh) + _dot(hl, wh) + _dot(hh, wl)
    lane = lax.broadcasted_iota(jnp.int32, logits.shape, 1)
    logits = jnp.where(lane < N_EXPERTS, logits, NEG_INF)
    m = jnp.max(logits, axis=-1, keepdims=True)
    e = jnp.exp(logits - m)
    aff_ref[...] = e / jnp.sum(e, axis=-1, keepdims=True)


def _norm_router(x2d, w, sc, sh, wr_pad):
    tm = 256
    per_b = S // tm
    return pl.pallas_call(
        _norm_router_kernel,
        grid=(T // tm,),
        in_specs=[pl.BlockSpec((tm, D), lambda i: (i, 0)),
                  pl.BlockSpec((1, D), lambda i: (0, 0)),
                  pl.BlockSpec((1, 1, D), lambda i: (i // per_b, 0, 0)),
                  pl.BlockSpec((1, 1, D), lambda i: (i // per_b, 0, 0)),
                  pl.BlockSpec((D, LANE), lambda i: (0, 0))],
        out_specs=[pl.BlockSpec((tm, D), lambda i: (i, 0)),
                   pl.BlockSpec((tm, LANE), lambda i: (i, 0))],
        out_shape=[jax.ShapeDtypeStruct((T, D), BF16),
                   jax.ShapeDtypeStruct((T, LANE), F32)],
        compiler_params=_cp(("arbitrary",), 40),
        name="norm2_router",
    )(x2d, w, sc, sh, wr_pad)


def _topk_kernel(a_ref, at_ref, u_ref, slot_ref, w_ref, rank_ref):
    rb = 256
    a_tok = at_ref[0]
    lane = lax.broadcasted_iota(jnp.int32, a_tok.shape, 1)
    ones = jnp.ones((16, rb), BF16)

    def count_greater(e, carry):
        a_row = a_ref[0, pl.ds(e, 1), :]
        a_col = jnp.sum(jnp.where(lane == e, a_tok, 0.0), axis=1, keepdims=True)
        r = jnp.zeros((16, S), F32)
        for blk in range(S // rb):
            greater = jnp.where(a_col[blk * rb:(blk + 1) * rb] > a_row, 1.0, 0.0).astype(BF16)
            r = r + _dot(ones, greater)
        rank_ref[pl.ds(e, 1), :] = r[0:1]
        return carry

    lax.fori_loop(0, N_EXPERTS, count_greater, 0)
    a = a_ref[0]
    in_top = rank_ref[...] < CAP
    n_top = jnp.sum(jnp.where(in_top, 1.0, 0.0), axis=1, keepdims=True)
    v = jnp.min(jnp.where(in_top, a, jnp.inf), axis=1, keepdims=True)
    tie = a == v
    n_tie = jnp.sum(jnp.where(tie, 1.0, 0.0), axis=1, keepdims=True)
    need = CAP - (n_top - n_tie)
    u = u_ref[...]
    tie_rank = _dot(jnp.where(tie, 1.0, 0.0).astype(BF16), u)
    sel = (in_top & (a > v)) | (tie & (tie_rank < need))
    slot = _dot(jnp.where(sel, 1.0, 0.0).astype(BF16), u)
    slot_ref[0] = jnp.where(sel, slot.astype(jnp.int32), -1)
    w_ref[0] = jnp.where(sel, a, 0.0)


def _topk(aff_t, aff3, upper):
    return pl.pallas_call(
        _topk_kernel,
        grid=(B,),
        in_specs=[pl.BlockSpec((1, N_EXPERTS, S), lambda b: (b, 0, 0)),
                  pl.BlockSpec((1, S, LANE), lambda b: (b, 0, 0)),
                  pl.BlockSpec((S, S), lambda b: (0, 0))],
        out_specs=[pl.BlockSpec((1, N_EXPERTS, S), lambda b: (b, 0, 0)),
                   pl.BlockSpec((1, N_EXPERTS, S), lambda b: (b, 0, 0))],
        out_shape=[jax.ShapeDtypeStruct((B, N_EXPERTS, S), jnp.int32),
                   jax.ShapeDtypeStruct((B, N_EXPERTS, S), F32)],
        scratch_shapes=[pltpu.VMEM((N_EXPERTS, S), F32)],
        compiler_params=_cp(("arbitrary",), 48),
        name="expert_topk",
    )(aff_t, aff3, upper)


def _gather_kernel(slot_ref, h_ref, xe_ref):
    hblk = h_ref[0]
    j = lax.broadcasted_iota(jnp.int32, (CAP, S), 0)
    for e in range(N_EXPERTS):
        p = jnp.where(j == slot_ref[0, e:e + 1, :], 1.0, 0.0).astype(BF16)
        xe_ref[e, 0] = _dot(p, hblk).astype(xe_ref.dtype)


def _gather(slot, h3):
    td = 1024
    return pl.pallas_call(
        _gather_kernel,
        grid=(B, D // td),
        in_specs=[pl.BlockSpec((1, N_EXPERTS, S), lambda b, j: (b, 0, 0)),
                  pl.BlockSpec((1, S, td), lambda b, j: (b, 0, j))],
        out_specs=pl.BlockSpec((N_EXPERTS, 1, CAP, td), lambda b, j: (0, b, 0, j)),
        out_shape=jax.ShapeDtypeStruct((N_EXPERTS, B, CAP, D), BF16),
        compiler_params=_cp(("arbitrary", "arbitrary"), 48),
        name="expert_gather",
    )(slot, h3)


def _gateup_kernel(x_ref, wg_ref, wu_ref, a_ref):
    x = x_ref[0]
    g = _dot(x, wg_ref[0].astype(BF16))
    u = _dot(x, wu_ref[0].astype(BF16))
    a_ref[0] = (_silu(g) * u).astype(a_ref.dtype)


def _gateup(xe, w_gate, w_up):
    tf = 256
    m = B * CAP
    return pl.pallas_call(
        _gateup_kernel,
        grid=(N_EXPERTS, FF // tf),
        in_specs=[pl.BlockSpec((1, m, D), lambda e, f: (e, 0, 0)),
                  pl.BlockSpec((1, D, tf), lambda e, f: (e, 0, f)),
                  pl.BlockSpec((1, D, tf), lambda e, f: (e, 0, f))],
        out_specs=pl.BlockSpec((1, m, tf), lambda e, f: (e, 0, f)),
        out_shape=jax.ShapeDtypeStruct((N_EXPERTS, m, FF), BF16),
        compiler_params=_cp(("arbitrary", "arbitrary"), 48),
        name="expert_gate_up",
    )(xe, w_gate, w_up)


def _down_kernel(a_ref, w_ref, o_ref):
    o_ref[0] = _dot(a_ref[0], w_ref[0].astype(BF16)).astype(o_ref.dtype)


def _down(a, w_down):
    tn = 1024
    m = B * CAP
    return pl.pallas_call(
        _down_kernel,
        grid=(N_EXPERTS, D // tn),
        in_specs=[pl.BlockSpec((1, m, FF), lambda e, j: (e, 0, 0)),
                  pl.BlockSpec((1, FF, tn), lambda e, j: (e, 0, j))],
        out_specs=pl.BlockSpec((1, m, tn), lambda e, j: (e, 0, j)),
        out_shape=jax.ShapeDtypeStruct((N_EXPERTS, m, D), BF16),
        compiler_params=_cp(("arbitrary", "arbitrary"), 48),
        name="expert_down",
    )(a, w_down)


def _scatter_kernel(slot_t_ref, w_t_ref, o_ref, x_ref, g_ref, y_ref, pw_ref):
    @pl.when(pl.program_id(1) == 0)
    def _():
        lane = lax.broadcasted_iota(jnp.int32, (S, CAP), 1)
        st = slot_t_ref[0]
        wt = w_t_ref[0]
        for e in range(N_EXPERTS):
            hit = lane == st[:, e:e + 1]
            pw_ref[:, e * CAP:(e + 1) * CAP] = jnp.where(hit, wt[:, e:e + 1], 0.0).astype(BF16)

    o = o_ref[:, 0].reshape(N_EXPERTS * CAP, o_ref.shape[-1])
    y_ref[0] = x_ref[0] + g_ref[0] * _dot(pw_ref[...], o)


def _scatter_residual(slot_t, w_t, out4, x3, gate2):
    tn = 512
    return pl.pallas_call(
        _scatter_kernel,
        grid=(B, D // tn),
        in_specs=[pl.BlockSpec((1, S, N_EXPERTS), lambda b, j: (b, 0, 0)),
                  pl.BlockSpec((1, S, N_EXPERTS), lambda b, j: (b, 0, 0)),
                  pl.BlockSpec((N_EXPERTS, 1, CAP, tn), lambda b, j: (0, b, 0, j)),
                  pl.BlockSpec((1, S, tn), lambda b, j: (b, 0, j)),
                  pl.BlockSpec((1, 1, tn), lambda b, j: (b, 0, j))],
        out_specs=pl.BlockSpec((1, S, tn), lambda b, j: (b, 0, j)),
        out_shape=jax.ShapeDtypeStruct((B, S, D), F32),
        scratch_shapes=[pltpu.VMEM((S, N_EXPERTS * CAP), BF16)],
        compiler_params=_cp(("arbitrary", "arbitrary"), 56),
        name="expert_scatter",
    )(slot_t, w_t, out4, x3, gate2)


def _final_kernel(x_ref, w_ref, o_ref):
    x = x_ref[...]
    ms = jnp.mean(x * x, axis=-1, keepdims=True)
    o_ref[...] = x * lax.rsqrt(ms + EPS) * w_ref[...]


def _final(x2d, w):
    tm = 256
    return pl.pallas_call(
        _final_kernel,
        grid=(T // tm,),
        in_specs=[pl.BlockSpec((tm, D), lambda i: (i, 0)),
                  pl.BlockSpec((1, D), lambda i: (0, 0))],
        out_specs=pl.BlockSpec((tm, D), lambda i: (i, 0)),
        out_shape=jax.ShapeDtypeStruct((T, D), F32),
        compiler_params=_cp(("arbitrary",), 40),
        name="final_norm",
    )(x2d, w)


def _rope_tables():
    half = A_HEAD // 2
    inv_freq = ROPE_THETA ** (-jnp.arange(half, dtype=F32) / half)
    ang = jnp.arange(S, dtype=F32)[:, None] * inv_freq[None, :]
    cos = jnp.cos(ang)
    sin = jnp.sin(ang)
    cos2 = jnp.concatenate([cos, cos], axis=-1)
    sin2 = jnp.concatenate([-sin, sin], axis=-1)
    scale = A_HEAD ** -0.5
    return jnp.stack([cos2 * scale, cos2]), jnp.stack([sin2 * scale, sin2])


def kernel(x, c, norm1_w, norm2_w, normf_w, w_ada, b_ada, w_in, conv_w, conv_b, dt_bias_f, dt_bias_b,
           a_log_f, a_log_b, d_skip, ssm_norm_w, w_ssm_out, w_attn_out, w_o, w_router, w_gate_e, w_up_e,
           w_down_e):
    assert x.shape == (B, S, D) and c.shape == (B, D) and w_in.shape[0] == 1
    x2d = x.reshape(T, D)
    layer = 0

    c8 = jnp.pad(c, ((0, 8 - B), (0, 0)))
    mod = _ada(c8, w_ada[layer], b_ada[layer][None, :])[:B]
    shift1, scale1, gate1, shift2, scale2, gate2 = [m[:, None, :] for m in jnp.split(mod, 6, axis=-1)]

    dilations = tuple(d for _, d in PATTERNS)
    h1_all = _norm_mod(x2d, norm1_w[layer][None, :], scale1, shift1, dilations[1:])
    h1 = h1_all[0]

    w_t = jnp.swapaxes(w_in[layer], 0, 1)
    zx = _mm(h1, w_t, col0=0, n=OFF_DT, tm=1024, tn=512, out_dtype=BF16, vmem_mb=48, name="proj_z_xbc")
    dt, dt_t = _dt_proj(h1, w_t)
    w_qkvg = _repack_weights(w_t)
    cos_tab, sin_tab = _rope_tables()
    gates = _mm(h1, w_qkvg, col0=3 * A_WIDTH, n=2 * D, tm=1024, tn=1024, out_dtype=BF16, vmem_mb=48,
                name="proj_gates")

    xbc = _conv_silu(zx.reshape(B, S, OFF_DT), conv_w[layer], conv_b[layer][None, :]).reshape(T, CONV_DIM)
    bias = jnp.concatenate([dt_bias_f[layer], dt_bias_b[layer]])
    alog = jnp.concatenate([a_log_f[layer], a_log_b[layer]])
    dsk = jnp.repeat(d_skip[layer], HEAD_P)[None, :]
    expand = (jnp.arange(D_INNER)[None, :] // HEAD_P == jnp.arange(N_HEADS)[:, None]).astype(BF16)
    expand = jnp.concatenate([expand, expand], axis=0)
    yf, yb = _ssd(xbc, dt, dt_t, bias[None, :], bias[:, None], alog[None, :], alog[:, None], dsk, expand)
    yn = _gated_norm(yf, yb, zx, ssm_norm_w[layer][None, :])

    outs, lses = [], []
    for g, dil in enumerate(dilations):
        length = S // dil
        hb, rb = {1: (4, 1), 4: (8, 2), 16: (8, 8)}[dil]
        to_group_order = lambda t: t.reshape(2, length, dil, A_HEAD).transpose(0, 2, 1, 3).reshape(2, S, A_HEAD)
        qkv_g = _qkv_group(h1_all[g].reshape(T, D), w_qkvg, to_group_order(cos_tab), to_group_order(sin_tab), g)
        o_g, lse_g = _attention_group(qkv_g.reshape(B, dil, length, 3 * A_OUT), g, dil, hb, rb)
        outs.append(o_g.transpose(0, 2, 1, 3).reshape(T, A_OUT))
        lses.append(lse_g.transpose(0, 3, 1, 2, 4).reshape(T, A_HPG))
    oc = _combine_groups(outs[0], outs[1], outs[2], jnp.concatenate(lses, axis=-1))

    merged = _merge(yn, oc, w_ssm_out[layer], w_attn_out[layer], gates)
    x1 = _wo_residual(merged, w_o[layer], x2d, gate1)

    wr_pad = jnp.pad(w_router[layer], ((0, 0), (0, LANE - N_EXPERTS)))
    h2, aff = _norm_router(x1, norm2_w[layer][None, :], scale2, shift2, wr_pad)
    aff_t = aff[:, :N_EXPERTS].reshape(B, S, N_EXPERTS).transpose(0, 2, 1)
    upper = (jnp.arange(S)[:, None] < jnp.arange(S)[None, :]).astype(BF16)
    slot, wsel = _topk(aff_t, aff.reshape(B, S, LANE), upper)
    xe = _gather(slot, h2.reshape(B, S, D))
    act = _gateup(xe.reshape(N_EXPERTS, B * CAP, D), w_gate_e[layer], w_up_e[layer])
    out_e = _down(act, w_down_e[layer])
    x2 = _scatter_residual(slot.transpose(0, 2, 1), wsel.transpose(0, 2, 1),
                           out_e.reshape(N_EXPERTS, B, CAP, D), x1.reshape(B, S, D), gate2)

    out = _final(x2.reshape(T, D), normf_w[None, :])
    return out.reshape(B, S, D)
```

```python
import functools
import math

import jax
import jax.numpy as jnp
from jax import lax
from jax.experimental import pallas as pl
from jax.experimental.pallas import tpu as pltpu

F32 = jnp.float32
BF16 = jnp.bfloat16

D = 4096
B = 4
S = 2048
T = B * S
HEAD_P = 64
D_INNER = D // 2
N_HEADS = D_INNER // HEAD_P
HEADS_PER_GROUP = 4
N_GROUPS = N_HEADS // HEADS_PER_GROUP
N_STATE = 128
CONV_K = 5
CHUNK = 128
N_CHUNKS = S // CHUNK
CONV_DIM = D_INNER + 2 * N_GROUPS * N_STATE
PATTERNS = ((128, 1), (512, 4), (2048, 16))
A_HEAD = 128
A_HPG = D // 512
A_WIDTH = len(PATTERNS) * A_HPG * A_HEAD
A_OUT = A_HPG * A_HEAD
ROPE_THETA = 10000.0
NEG_INF = -1e30
N_EXPERTS = 16
FF = D // 2
CAP = 2 * S // N_EXPERTS
EPS = 1e-6
IN_SIZES = (D_INNER, CONV_DIM, 2 * N_HEADS, A_WIDTH, A_WIDTH, A_WIDTH, 2 * D)
OFF_DT = D_INNER + CONV_DIM
OFF_Q = OFF_DT + 2 * N_HEADS
N_QKVG = 3 * A_WIDTH + 2 * D
VG_W = A_WIDTH + 2 * D
LANE = 128


def _cp(sem, vmem_mb=None):
    kw = dict(dimension_semantics=sem)
    if vmem_mb is not None:
        kw["vmem_limit_bytes"] = vmem_mb << 20
    return pltpu.CompilerParams(**kw)


def _dot(a, b):
    return jnp.dot(a, b, preferred_element_type=F32)


def _dot_nt(a, b):
    return lax.dot_general(a, b, (((1,), (1,)), ((), ())), preferred_element_type=F32)


def _split2(v):
    h = v.astype(BF16)
    l = (v - h.astype(F32)).astype(BF16)
    return h, l


def _split3(v):
    h = v.astype(BF16)
    r = v - h.astype(F32)
    m = r.astype(BF16)
    l = (r - m.astype(F32)).astype(BF16)
    return h, m, l


def _silu(v):
    return v * jax.nn.sigmoid(v)


def _softplus(v):
    return jnp.maximum(v, 0.0) + jnp.log1p(jnp.exp(-jnp.abs(v)))


def _ada_kernel(c_ref, w_ref, b_ref, o_ref):
    ca = _silu(c_ref[...]).astype(BF16)
    o_ref[...] = _dot(ca, w_ref[...].astype(BF16)) + b_ref[...]


def _ada(c8, w, b, n):
    tn = 512
    return pl.pallas_call(
        _ada_kernel,
        grid=(n // tn,),
        in_specs=[pl.BlockSpec((8, D), lambda j: (0, 0)),
                  pl.BlockSpec((D, tn), lambda j: (0, j)),
                  pl.BlockSpec((1, tn), lambda j: (0, j))],
        out_specs=pl.BlockSpec((8, tn), lambda j: (0, j)),
        out_shape=jax.ShapeDtypeStruct((8, n), F32),
        compiler_params=_cp(("arbitrary",), 40),
        name="ada_mod",
    )(c8, w, b)


def _norm_mod_kernel(x_ref, w_ref, sc_ref, sh_ref, o_ref, *strided_refs, dilations):
    w = w_ref[...]
    sc = 1.0 + sc_ref[0]
    sh = sh_ref[0]

    x = x_ref[...]
    ms = jnp.mean(x * x, axis=-1, keepdims=True)
    hn = (x * lax.rsqrt(ms + EPS) * w * sc + sh).astype(o_ref.dtype)
    o_ref[...] = hn
    tm = x.shape[0]
    row = lax.broadcasted_iota(jnp.int32, (tm, tm), 0)
    col = lax.broadcasted_iota(jnp.int32, (tm, tm), 1)
    for d, s_ref in zip(dilations, strided_refs):
        per = tm // d
        src = (row & (per - 1)) * d + lax.shift_right_logical(row, per.bit_length() - 1)
        perm = jnp.where(col == src, 1.0, 0.0).astype(BF16)
        res = _dot(perm, hn).astype(o_ref.dtype)
        for r in range(d):
            s_ref[0, r] = res[r * per:(r + 1) * per]


def _norm_mod(x2d, w, sc, sh, dilations):
    tm = 256
    per_b = S // tm
    out_specs = [pl.BlockSpec((tm, D), lambda i: (i, 0))]
    out_shape = [jax.ShapeDtypeStruct((T, D), BF16)]
    for d in dilations:
        out_specs.append(pl.BlockSpec((1, d, tm // d, D), lambda i: (i // per_b, 0, i % per_b, 0)))
        out_shape.append(jax.ShapeDtypeStruct((B, d, S // d, D), BF16))
    return pl.pallas_call(
        functools.partial(_norm_mod_kernel, dilations=dilations),
        grid=(T // tm,),
        in_specs=[pl.BlockSpec((tm, D), lambda i: (i, 0)),
                  pl.BlockSpec((1, D), lambda i: (0, 0)),
                  pl.BlockSpec((1, 1, D), lambda i: (i // per_b, 0, 0)),
                  pl.BlockSpec((1, 1, D), lambda i: (i // per_b, 0, 0))],
        out_specs=out_specs,
        out_shape=out_shape,
        compiler_params=_cp(("arbitrary",), 48),
        name="norm1_mod",
    )(x2d, w, sc, sh)


def _mm_kernel(a_ref, w_ref, o_ref, *scratch, cast_w):
    if cast_w:
        (wb_ref,) = scratch

        @pl.when(pl.program_id(1) == 0)
        def _():
            wb_ref[...] = w_ref[...].T.astype(BF16)

        w = wb_ref[...]
    else:
        w = w_ref[...]
    o_ref[...] = _dot(a_ref[...], w).astype(o_ref.dtype)


def _mm(a, w, *, col0, n, tm, tn, out_dtype, vmem_mb, name):
    m, k = a.shape
    cast_w = w.dtype != BF16
    cb = col0 // tn
    scratch = [pltpu.VMEM((k, tn), BF16)] if cast_w else []
    if cast_w:
        w_spec = pl.BlockSpec((tn, k), lambda j, i: (cb + j, 0))
    else:
        w_spec = pl.BlockSpec((k, tn), lambda j, i: (0, cb + j))
    return pl.pallas_call(
        functools.partial(_mm_kernel, cast_w=cast_w),
        grid=(n // tn, m // tm),
        in_specs=[pl.BlockSpec((tm, k), lambda j, i: (i, 0)),
                  w_spec],
        out_specs=pl.BlockSpec((tm, tn), lambda j, i: (i, j)),
        out_shape=jax.ShapeDtypeStruct((m, n), out_dtype),
        scratch_shapes=scratch,
        compiler_params=_cp(("arbitrary", "arbitrary"), vmem_mb),
        name=name,
    )(a, w)


def _mm_mod_kernel(a_ref, w_ref, c_ref, wa_ref, ba_ref, o_ref, mod_ref):
    o_ref[...] = _dot(a_ref[...], w_ref[...]).astype(o_ref.dtype)
    ca = _silu(c_ref[...]).astype(BF16)
    mod_ref[...] = _dot(ca, wa_ref[...].astype(BF16)) + ba_ref[...]


def _mm_with_modulation(a, w, c8, w_ada, b_ada, *, col0, n, mod_col0, tm, tn):
    m, k = a.shape
    cb = col0 // tn
    steps_i = m // tm
    n_mod = w_ada.shape[1] - mod_col0
    ta = n_mod // ((n // tn) * steps_i)
    assert ta % LANE == 0 and ta * (n // tn) * steps_i == n_mod and mod_col0 % ta == 0
    mb = mod_col0 // ta
    return pl.pallas_call(
        _mm_mod_kernel,
        grid=(n // tn, steps_i),
        in_specs=[pl.BlockSpec((tm, k), lambda j, i: (i, 0)),
                  pl.BlockSpec((k, tn), lambda j, i: (0, cb + j)),
                  pl.BlockSpec((8, D), lambda j, i: (0, 0)),
                  pl.BlockSpec((D, ta), lambda j, i: (0, mb + j * steps_i + i)),
                  pl.BlockSpec((1, ta), lambda j, i: (0, mb + j * steps_i + i))],
        out_specs=[pl.BlockSpec((tm, tn), lambda j, i: (i, j)),
                   pl.BlockSpec((8, ta), lambda j, i: (0, j * steps_i + i))],
        out_shape=[jax.ShapeDtypeStruct((m, n), BF16),
                   jax.ShapeDtypeStruct((8, n_mod), F32)],
        compiler_params=_cp(("arbitrary", "arbitrary"), 60),
        name="proj_gates_mod",
    )(a, w, c8, w_ada, b_ada)


def _repack_kernel(a_ref, o_ref):
    o_ref[...] = a_ref[...].T.astype(o_ref.dtype)


def _repack_weights(w_t):
    tk, tn = 1024, 1024
    n_tiles = N_QKVG // tn
    n_qkv = 3 * A_WIDTH // tn
    n_groups = len(PATTERNS)

    def out_map(m, kk):
        return (kk, jnp.where(m < n_qkv, (m % n_groups) * 3 + m // n_groups, m))

    return pl.pallas_call(
        _repack_kernel,
        grid=(n_tiles, D // tk),
        in_specs=[pl.BlockSpec((pl.Element(tn), pl.Element(tk)),
                               lambda m, kk: (pl.multiple_of(OFF_Q + m * tn, math.gcd(OFF_Q, tn)),
                                              pl.multiple_of(kk * tk, tk)))],
        out_specs=pl.BlockSpec((tk, tn), out_map),
        out_shape=jax.ShapeDtypeStruct((D, N_QKVG), BF16),
        compiler_params=_cp(("arbitrary", "arbitrary"), 40),
        name="repack_w_qkvg",
    )(w_t)


def _qkv_kernel(a_ref, w_ref, cos_ref, sin_ref, o_ref):
    acc = _dot(a_ref[...], w_ref[...])
    j = pl.program_id(0)

    @pl.when(j < 2)
    def _():
        cos = cos_ref[0]
        sin = sin_ref[0]
        for h in range(acc.shape[1] // A_HEAD):
            t = acc[:, h * A_HEAD:(h + 1) * A_HEAD]
            rot = t * cos + pltpu.roll(t, A_HEAD // 2, 1) * sin
            o_ref[:, h * A_HEAD:(h + 1) * A_HEAD] = rot.astype(o_ref.dtype)

    @pl.when(j == 2)
    def _():
        o_ref[...] = acc.astype(o_ref.dtype)


def _qkv_group(a, w_all, cos_tab, sin_tab, g):
    tm, tn = 1024, 1024
    per_b = S // tm
    return pl.pallas_call(
        _qkv_kernel,
        grid=(3, T // tm),
        in_specs=[pl.BlockSpec((tm, D), lambda j, i: (i, 0)),
                  pl.BlockSpec((D, tn), lambda j, i: (0, 3 * g + j)),
                  pl.BlockSpec((1, tm, A_HEAD), lambda j, i: (jnp.minimum(j, 1), i % per_b, 0)),
                  pl.BlockSpec((1, tm, A_HEAD), lambda j, i: (jnp.minimum(j, 1), i % per_b, 0))],
        out_specs=pl.BlockSpec((tm, tn), lambda j, i: (i, j)),
        out_shape=jax.ShapeDtypeStruct((T, 3 * tn), BF16),
        compiler_params=_cp(("arbitrary", "arbitrary"), 48),
        name=f"proj_qkv_g{g}",
    )(a, w_all, cos_tab, sin_tab)


def _dt_kernel(a_ref, w_ref, o_ref, ot_ref):
    ot = _dot_nt(w_ref[...].astype(BF16), a_ref[...])
    ot_ref[...] = ot[0:2 * N_HEADS, :]
    o_ref[...] = ot.T


def _dt_proj(a, w_t):
    tm = 1024
    return pl.pallas_call(
        _dt_kernel,
        grid=(T // tm,),
        in_specs=[pl.BlockSpec((tm, D), lambda i: (i, 0)),
                  pl.BlockSpec((LANE, D), lambda i: (OFF_DT // LANE, 0))],
        out_specs=[pl.BlockSpec((tm, LANE), lambda i: (i, 0)),
                   pl.BlockSpec((2 * N_HEADS, tm), lambda i: (0, i))],
        out_shape=[jax.ShapeDtypeStruct((T, LANE), F32),
                   jax.ShapeDtypeStruct((2 * N_HEADS, T), F32)],
        compiler_params=_cp(("arbitrary",), 40),
        name="proj_dt",
    )(a, w_t)


def _conv_kernel(x_ref, w_ref, b_ref, o_ref):
    x = x_ref[0].astype(F32)
    n = x.shape[0]
    row = lax.broadcasted_iota(jnp.int32, x.shape, 0)
    acc = x * w_ref[CONV_K // 2:CONV_K // 2 + 1, :] + b_ref[...]
    for k in range(CONV_K):
        off = k - CONV_K // 2
        if off == 0:
            continue
        shifted = pltpu.roll(x, (-off) % n, 0)
        ok = (row + off >= 0) & (row + off < n)
        acc = acc + jnp.where(ok, shifted, 0.0) * w_ref[k:k + 1, :]
    o_ref[0] = _silu(acc).astype(o_ref.dtype)


def _conv_silu(zx3, conv_w, conv_b):
    tc = 512
    c0 = D_INNER // tc
    return pl.pallas_call(
        _conv_kernel,
        grid=(B, CONV_DIM // tc),
        in_specs=[pl.BlockSpec((1, S, tc), lambda b, j: (b, 0, c0 + j)),
                  pl.BlockSpec((CONV_K, tc), lambda b, j: (0, j)),
                  pl.BlockSpec((1, tc), lambda b, j: (0, j))],
        out_specs=pl.BlockSpec((1, S, tc), lambda b, j: (b, 0, j)),
        out_shape=jax.ShapeDtypeStruct((B, S, CONV_DIM), BF16),
        compiler_params=_cp(("arbitrary", "arbitrary"), 48),
        name="conv_silu",
    )(zx3, conv_w, conv_b)


def _ssd_direction(x_ref, y_ref, st_ref, dt_raw, dtT_raw, bias, biasT, alog, alogT, e_ref, dsk, reverse):
    tc = CHUNK
    gw = HEADS_PER_GROUP * HEAD_P
    dt = _softplus(dt_raw + bias)
    dtT = _softplus(dtT_raw + biasT)
    da = dt * (-jnp.exp(alog))
    daT = dtT * (-jnp.exp(alogT))
    r = lax.broadcasted_iota(jnp.int32, (tc, tc), 0)
    c = lax.broadcasted_iota(jnp.int32, (tc, tc), 1)
    keep = (c >= r) if reverse else (c <= r)
    tri = jnp.where(keep, 1.0, 0.0).astype(BF16)
    triT = jnp.where((r >= c) if reverse else (r <= c), 1.0, 0.0).astype(BF16)
    h3 = _split3(da)
    cs = _dot(tri, h3[0]) + _dot(tri, h3[1]) + _dot(tri, h3[2])
    t3 = _split3(daT)
    csT = _dot(t3[0], triT) + _dot(t3[1], triT) + _dot(t3[2], triT)
    tot = cs[0:1, :] if reverse else cs[tc - 1:tc, :]
    def hi_lo(v):
        h = v.astype(BF16).astype(F32)
        return jnp.concatenate([h, v - h], axis=1).astype(BF16)

    dt2 = hi_lo(dt)
    ed2 = hi_lo(jnp.exp(tot - cs))
    ec2 = hi_lo(jnp.exp(cs))
    et2 = hi_lo(jnp.broadcast_to(jnp.exp(tot), (16, N_HEADS)))
    lanehead = lax.shift_right_logical(lax.broadcasted_iota(jnp.int32, (tc, gw), 1), HEAD_P.bit_length() - 1)
    for g in range(N_GROUPS):
        eg = e_ref[:, g * gw:(g + 1) * gw]
        dt_e = _dot(dt2, eg)
        edec_e = _dot(ed2, eg)
        ecs_e = _dot(ec2, eg)
        etot_e = _dot(et2, eg)[0:1, :]
        xs = x_ref[:, g * gw:(g + 1) * gw].astype(F32)
        bg = x_ref[:, D_INNER + g * N_STATE:D_INNER + (g + 1) * N_STATE]
        cg = x_ref[:, D_INNER + N_GROUPS * N_STATE + g * N_STATE:D_INNER + N_GROUPS * N_STATE + (g + 1) * N_STATE]
        xd = xs * dt_e
        xdd = (xd * edec_e).astype(BF16)
        cb = _dot_nt(cg, bg)
        ms = []
        rs = []
        for k in range(HEADS_PER_GROUP):
            h = HEADS_PER_GROUP * g + k
            decay = jnp.where(keep, jnp.exp(cs[:, h:h + 1] - csT[h:h + 1, :]), 0.0)
            ms.append((cb * decay).astype(BF16))
            rs.append(jnp.where(lanehead == k, xd, 0.0).astype(BF16))
        lhs = jnp.concatenate(ms, axis=1)
        rhs = jnp.concatenate(rs, axis=0)
        sg = st_ref[g]
        y = _dot(lhs, rhs) + _dot(cg, sg.astype(BF16)) * ecs_e
        if dsk is not None:
            y = y + xs * dsk[:, g * gw:(g + 1) * gw]
        y_ref[:, g * gw:(g + 1) * gw] = y.astype(y_ref.dtype)
        bgT = bg.astype(F32).T.astype(BF16)
        st_ref[g] = sg * etot_e + _dot(bgT, xdd)


def _ssd_kernel(xf_ref, xb_ref, dtf_ref, dtb_ref, dtTf_ref, dtTb_ref, bias_ref, biasT_ref,
                alog_ref, alogT_ref, dsk_ref, e_ref, yf_ref, yb_ref, sf_ref, sb_ref):
    @pl.when(pl.program_id(1) == 0)
    def _():
        sf_ref[...] = jnp.zeros_like(sf_ref)
        sb_ref[...] = jnp.zeros_like(sb_ref)

    nh = N_HEADS
    _ssd_direction(xf_ref, yf_ref, sf_ref, dtf_ref[:, 0:nh], dtTf_ref[0:nh, :], bias_ref[:, 0:nh],
                   biasT_ref[0:nh, :], alog_ref[:, 0:nh], alogT_ref[0:nh, :], e_ref, dsk_ref[...], False)
    _ssd_direction(xb_ref, yb_ref, sb_ref, dtb_ref[:, nh:2 * nh], dtTb_ref[nh:2 * nh, :], bias_ref[:, nh:2 * nh],
                   biasT_ref[nh:2 * nh, :], alog_ref[:, nh:2 * nh], alogT_ref[nh:2 * nh, :], e_ref, None, True)


def _ssd(xbc, dt, dtT, bias, biasT, alog, alogT, dsk, expand):
    nc = N_CHUNKS
    fwd = lambda b, c: (b * nc + c, 0)
    bwd = lambda b, c: (b * nc + nc - 1 - c, 0)
    fwdT = lambda b, c: (0, b * nc + c)
    bwdT = lambda b, c: (0, b * nc + nc - 1 - c)
    const = lambda b, c: (0, 0)
    gw = HEADS_PER_GROUP * HEAD_P
    return pl.pallas_call(
        _ssd_kernel,
        grid=(B, nc),
        in_specs=[pl.BlockSpec((CHUNK, CONV_DIM), fwd),
                  pl.BlockSpec((CHUNK, CONV_DIM), bwd),
                  pl.BlockSpec((CHUNK, LANE), fwd),
                  pl.BlockSpec((CHUNK, LANE), bwd),
                  pl.BlockSpec((2 * N_HEADS, CHUNK), fwdT),
                  pl.BlockSpec((2 * N_HEADS, CHUNK), bwdT),
                  pl.BlockSpec((1, 2 * N_HEADS), const),
                  pl.BlockSpec((2 * N_HEADS, 1), const),
                  pl.BlockSpec((1, 2 * N_HEADS), const),
                  pl.BlockSpec((2 * N_HEADS, 1), const),
                  pl.BlockSpec((1, D_INNER), const),
                  pl.BlockSpec((2 * N_HEADS, D_INNER), const)],
        out_specs=[pl.BlockSpec((CHUNK, D_INNER), fwd),
                   pl.BlockSpec((CHUNK, D_INNER), bwd)],
        out_shape=[jax.ShapeDtypeStruct((T, D_INNER), BF16),
                   jax.ShapeDtypeStruct((T, D_INNER), BF16)],
        scratch_shapes=[pltpu.VMEM((N_GROUPS, N_STATE, gw), F32),
                        pltpu.VMEM((N_GROUPS, N_STATE, gw), F32)],
        compiler_params=_cp(("arbitrary", "arbitrary"), 40),
        name="ssd_scan",
    )(xbc, xbc, dt, dt, dtT, dtT, bias, biasT, alog, alogT, dsk, expand)


def _gnorm_kernel(yf_ref, yb_ref, z_ref, w_ref, o_ref):
    z = z_ref[...].astype(F32)
    y = (yf_ref[...].astype(F32) + yb_ref[...].astype(F32)) * _silu(z)
    ms = jnp.mean(y * y, axis=-1, keepdims=True)
    o_ref[...] = (y * lax.rsqrt(ms + EPS) * w_ref[...]).astype(o_ref.dtype)


def _gated_norm(yf, yb, zx, w):
    tm = 512
    return pl.pallas_call(
        _gnorm_kernel,
        grid=(T // tm,),
        in_specs=[pl.BlockSpec((tm, D_INNER), lambda i: (i, 0)),
                  pl.BlockSpec((tm, D_INNER), lambda i: (i, 0)),
                  pl.BlockSpec((tm, D_INNER), lambda i: (i, 0)),
                  pl.BlockSpec((1, D_INNER), lambda i: (0, 0))],
        out_specs=pl.BlockSpec((tm, D_INNER), lambda i: (i, 0)),
        out_shape=jax.ShapeDtypeStruct((T, D_INNER), BF16),
        compiler_params=_cp(("arbitrary",), 40),
        name="ssm_gated_norm",
    )(yf, yb, zx, w)


def _attn_kernel(q_ref, k_ref, v_ref, o_ref, lse_ref, *, length, hb, rb, radius):
    tq = 128
    win = tq + 2 * radius
    pair = length < win
    assert (not pair) or (2 * length == win and rb % 2 == 0)
    for rr in range(rb):
        for h in range(hb):
            cols = slice(h * A_HEAD, (h + 1) * A_HEAD)
            for qb in range(length // tq):
                q0 = qb * tq
                q = q_ref[0, rr, q0:q0 + tq, cols]
                if pair:
                    r0 = rr - rr % 2
                    ws = -(rr % 2) * length
                    k = k_ref[0, r0:r0 + 2, :, cols].reshape(win, A_HEAD)
                    v = v_ref[0, r0:r0 + 2, :, cols].reshape(win, A_HEAD)
                else:
                    ws = min(max(q0 - radius, 0), length - win)
                    k = k_ref[0, rr, ws:ws + win, cols]
                    v = v_ref[0, rr, ws:ws + win, cols]
                s = _dot_nt(q, k)
                ti = q0 + lax.broadcasted_iota(jnp.int32, (tq, win), 0)
                tj = ws + lax.broadcasted_iota(jnp.int32, (tq, win), 1)
                ok = (tj - ti <= radius) & (ti - tj <= radius)
                if pair:
                    ok = ok & (tj >= 0) & (tj < length)
                s = jnp.where(ok, s, NEG_INF)
                m = jnp.max(s, axis=-1, keepdims=True)
                p = jnp.exp(s - m)
                l = jnp.sum(p, axis=-1, keepdims=True)
                o = _dot(p.astype(BF16), v) / l
                o_ref[0, rr, q0:q0 + tq, cols] = o.astype(o_ref.dtype)
                lse_ref[0, rr, 0, q0:q0 + tq, h:h + 1] = m + jnp.log(l)


def _attention_group(qkv4, g, dil, hb, rb):
    length = S // dil
    window = PATTERNS[g][0]
    radius = window // (2 * dil)
    bw = hb * A_HEAD
    nj = A_HPG // hb
    return pl.pallas_call(
        functools.partial(_attn_kernel, length=length, hb=hb, rb=rb, radius=radius),
        grid=(B, dil // rb, nj),
        in_specs=[pl.BlockSpec((1, rb, length, bw), lambda b, r, j: (b, r, 0, j)),
                  pl.BlockSpec((1, rb, length, bw), lambda b, r, j: (b, r, 0, nj + j)),
                  pl.BlockSpec((1, rb, length, bw), lambda b, r, j: (b, r, 0, 2 * nj + j))],
        out_specs=[pl.BlockSpec((1, rb, length, bw), lambda b, r, j: (b, r, 0, j)),
                   pl.BlockSpec((1, rb, 1, length, hb), lambda b, r, j: (b, r, j, 0, 0))],
        out_shape=[jax.ShapeDtypeStruct((B, dil, length, A_OUT), BF16),
                   jax.ShapeDtypeStruct((B, dil, nj, length, hb), F32)],
        compiler_params=_cp(("arbitrary", "arbitrary", "arbitrary"), 40),
        name=f"attn_dil{dil}",
    )(qkv4, qkv4, qkv4)


def _combine_kernel(o0_ref, o1_ref, o2_ref, lse_ref, out_ref):
    lse = lse_ref[...]
    for hh in range(A_HPG):
        l0 = lse[:, hh:hh + 1]
        l1 = lse[:, A_HPG + hh:A_HPG + hh + 1]
        l2 = lse[:, 2 * A_HPG + hh:2 * A_HPG + hh + 1]
        m = jnp.maximum(jnp.maximum(l0, l1), l2)
        e0 = jnp.exp(l0 - m)
        e1 = jnp.exp(l1 - m)
        e2 = jnp.exp(l2 - m)
        inv = 1.0 / (e0 + e1 + e2)
        cols = slice(hh * A_HEAD, (hh + 1) * A_HEAD)
        mix = ((e0 * inv) * o0_ref[:, cols].astype(F32) + (e1 * inv) * o1_ref[:, cols].astype(F32)
               + (e2 * inv) * o2_ref[:, cols].astype(F32))
        out_ref[:, cols] = mix.astype(out_ref.dtype)


def _combine_groups(o0, o1, o2, lse):
    tm = 512
    n_l = lse.shape[1]
    return pl.pallas_call(
        _combine_kernel,
        grid=(T // tm,),
        in_specs=[pl.BlockSpec((tm, A_OUT), lambda i: (i, 0)),
                  pl.BlockSpec((tm, A_OUT), lambda i: (i, 0)),
                  pl.BlockSpec((tm, A_OUT), lambda i: (i, 0)),
                  pl.BlockSpec((tm, n_l), lambda i: (i, 0))],
        out_specs=pl.BlockSpec((tm, A_OUT), lambda i: (i, 0)),
        out_shape=jax.ShapeDtypeStruct((T, A_OUT), BF16),
        compiler_params=_cp(("arbitrary",), 40),
        name="attn_combine",
    )(o0, o1, o2, lse)


def _merge_kernel(ys_ref, oa_ref, ws_ref, wa_ref, gs_ref, ga_ref, o_ref, wsb_ref, wab_ref):
    @pl.when(pl.program_id(1) == 0)
    def _():
        wsb_ref[...] = ws_ref[...].astype(BF16)
        wab_ref[...] = wa_ref[...].astype(BF16)

    y_ssm = _dot(ys_ref[...], wsb_ref[...])
    y_att = _dot(oa_ref[...], wab_ref[...])
    g_s = jax.nn.sigmoid(gs_ref[...].astype(F32))
    g_a = jax.nn.sigmoid(ga_ref[...].astype(F32))
    o_ref[...] = (g_s * y_ssm + g_a * y_att).astype(o_ref.dtype)


def _merge(yn, oc, w_ssm_out, w_attn_out, gates):
    tm, tn = 1024, 512
    g0 = 0
    g1 = D // tn
    return pl.pallas_call(
        _merge_kernel,
        grid=(D // tn, T // tm),
        in_specs=[pl.BlockSpec((tm, D_INNER), lambda j, i: (i, 0)),
                  pl.BlockSpec((tm, A_OUT), lambda j, i: (i, 0)),
                  pl.BlockSpec((D_INNER, tn), lambda j, i: (0, j)),
                  pl.BlockSpec((A_OUT, tn), lambda j, i: (0, j)),
                  pl.BlockSpec((tm, tn), lambda j, i: (i, g0 + j)),
                  pl.BlockSpec((tm, tn), lambda j, i: (i, g1 + j))],
        out_specs=pl.BlockSpec((tm, tn), lambda j, i: (i, j)),
        out_shape=jax.ShapeDtypeStruct((T, D), BF16),
        scratch_shapes=[pltpu.VMEM((D_INNER, tn), BF16), pltpu.VMEM((A_OUT, tn), BF16)],
        compiler_params=_cp(("arbitrary", "arbitrary"), 48),
        name="branch_merge",
    )(yn, oc, w_ssm_out, w_attn_out, gates, gates)


def _wo_kernel(a_ref, w_ref, x_ref, g_ref, o_ref, wb_ref):
    @pl.when(pl.program_id(1) == 0)
    def _():
        wb_ref[...] = w_ref[...].astype(BF16)

    o_ref[...] = x_ref[...] + g_ref[0] * _dot(a_ref[...], wb_ref[...])


def _wo_residual(merged, w_o, x2d, gate1):
    tm, tn = 1024, 512
    per_b = S // tm
    return pl.pallas_call(
        _wo_kernel,
        grid=(D // tn, T // tm),
        in_specs=[pl.BlockSpec((tm, D), lambda j, i: (i, 0)),
                  pl.BlockSpec((D, tn), lambda j, i: (0, j)),
                  pl.BlockSpec((tm, tn), lambda j, i: (i, j)),
                  pl.BlockSpec((1, 1, tn), lambda j, i: (i // per_b, 0, j))],
        out_specs=pl.BlockSpec((tm, tn), lambda j, i: (i, j)),
        out_shape=jax.ShapeDtypeStruct((T, D), F32),
        scratch_shapes=[pltpu.VMEM((D, tn), BF16)],
        compiler_params=_cp(("arbitrary", "arbitrary"), 52),
        name="out_proj_residual",
    )(merged, w_o, x2d, gate1)


def _norm_router_kernel(x_ref, w_ref, sc_ref, sh_ref, wr_ref, h_ref, aff_ref):
    x = x_ref[...]
    ms = jnp.mean(x * x, axis=-1, keepdims=True)
    h = x * lax.rsqrt(ms + EPS) * w_ref[...] * (1.0 + sc_ref[0]) + sh_ref[0]
    h_ref[...] = h.astype(h_ref.dtype)
    hh, hl = _split2(h)
    wh, wl = _split2(wr_ref[...])
    logits = _dot(hh, wh) + _dot(hl, wh) + _dot(hh, wl)
    lane = lax.broadcasted_iota(jnp.int32, logits.shape, 1)
    logits = jnp.where(lane < N_EXPERTS, logits, NEG_INF)
    m = jnp.max(logits, axis=-1, keepdims=True)
    e = jnp.exp(logits - m)
    aff_ref[...] = e / jnp.sum(e, axis=-1, keepdims=True)


def _norm_router(x2d, w, sc, sh, wr_pad):
    tm = 256
    per_b = S // tm
    return pl.pallas_call(
        _norm_router_kernel,
        grid=(T // tm,),
        in_specs=[pl.BlockSpec((tm, D), lambda i: (i, 0)),
                  pl.BlockSpec((1, D), lambda i: (0, 0)),
                  pl.BlockSpec((1, 1, D), lambda i: (i // per_b, 0, 0)),
                  pl.BlockSpec((1, 1, D), lambda i: (i // per_b, 0, 0)),
                  pl.BlockSpec((D, LANE), lambda i: (0, 0))],
        out_specs=[pl.BlockSpec((tm, D), lambda i: (i, 0)),
                   pl.BlockSpec((tm, LANE), lambda i: (i, 0))],
        out_shape=[jax.ShapeDtypeStruct((T, D), BF16),
                   jax.ShapeDtypeStruct((T, LANE), F32)],
        compiler_params=_cp(("arbitrary",), 40),
        name="norm2_router",
    )(x2d, w, sc, sh, wr_pad)


def _topk_kernel(a_ref, at_ref, u_ref, slot_ref, w_ref, rank_ref):
    rb = 256
    a_tok = at_ref[0]
    lane = lax.broadcasted_iota(jnp.int32, a_tok.shape, 1)
    ones = jnp.ones((16, rb), BF16)

    def count_greater(e, carry):
        a_row = a_ref[0, pl.ds(e, 1), :]
        a_col = jnp.sum(jnp.where(lane == e, a_tok, 0.0), axis=1, keepdims=True)
        r = jnp.zeros((16, S), F32)
        for blk in range(S // rb):
            greater = jnp.where(a_col[blk * rb:(blk + 1) * rb] > a_row, 1.0, 0.0).astype(BF16)
            r = r + _dot(ones, greater)
        rank_ref[pl.ds(e, 1), :] = r[0:1]
        return carry

    lax.fori_loop(0, N_EXPERTS, count_greater, 0)
    a = a_ref[0]
    in_top = rank_ref[...] < CAP
    n_top = jnp.sum(jnp.where(in_top, 1.0, 0.0), axis=1, keepdims=True)
    v = jnp.min(jnp.where(in_top, a, jnp.inf), axis=1, keepdims=True)
    tie = a == v
    n_tie = jnp.sum(jnp.where(tie, 1.0, 0.0), axis=1, keepdims=True)
    need = CAP - (n_top - n_tie)
    u = u_ref[...]
    tie_rank = _dot(jnp.where(tie, 1.0, 0.0).astype(BF16), u)
    sel = (in_top & (a > v)) | (tie & (tie_rank < need))
    slot = _dot(jnp.where(sel, 1.0, 0.0).astype(BF16), u)
    slot_ref[0] = jnp.where(sel, slot.astype(jnp.int32), -1)
    w_ref[0] = jnp.where(sel, a, 0.0)


def _topk(aff_t, aff3, upper):
    return pl.pallas_call(
        _topk_kernel,
        grid=(B,),
        in_specs=[pl.BlockSpec((1, N_EXPERTS, S), lambda b: (b, 0, 0)),
                  pl.BlockSpec((1, S, LANE), lambda b: (b, 0, 0)),
                  pl.BlockSpec((S, S), lambda b: (0, 0))],
        out_specs=[pl.BlockSpec((1, N_EXPERTS, S), lambda b: (b, 0, 0)),
                   pl.BlockSpec((1, N_EXPERTS, S), lambda b: (b, 0, 0))],
        out_shape=[jax.ShapeDtypeStruct((B, N_EXPERTS, S), jnp.int32),
                   jax.ShapeDtypeStruct((B, N_EXPERTS, S), F32)],
        scratch_shapes=[pltpu.VMEM((N_EXPERTS, S), F32)],
        compiler_params=_cp(("arbitrary",), 48),
        name="expert_topk",
    )(aff_t, aff3, upper)


def _gather_kernel(slot_ref, h_ref, xe_ref):
    hblk = h_ref[0]
    j = lax.broadcasted_iota(jnp.int32, (CAP, S), 0)
    for e in range(N_EXPERTS):
        p = jnp.where(j == slot_ref[0, e:e + 1, :], 1.0, 0.0).astype(BF16)
        xe_ref[e, 0] = _dot(p, hblk).astype(xe_ref.dtype)


def _gather(slot, h3):
    td = 1024
    return pl.pallas_call(
        _gather_kernel,
        grid=(B, D // td),
        in_specs=[pl.BlockSpec((1, N_EXPERTS, S), lambda b, j: (b, 0, 0)),
                  pl.BlockSpec((1, S, td), lambda b, j: (b, 0, j))],
        out_specs=pl.BlockSpec((N_EXPERTS, 1, CAP, td), lambda b, j: (0, b, 0, j)),
        out_shape=jax.ShapeDtypeStruct((N_EXPERTS, B, CAP, D), BF16),
        compiler_params=_cp(("arbitrary", "arbitrary"), 48),
        name="expert_gather",
    )(slot, h3)


def _gateup_kernel(x_ref, wg_ref, wu_ref, a_ref):
    x = x_ref[0]
    g = _dot(x, wg_ref[0].astype(BF16))
    u = _dot(x, wu_ref[0].astype(BF16))
    a_ref[0] = (_silu(g) * u).astype(a_ref.dtype)


def _gateup(xe, w_gate, w_up):
    tf = 256
    m = B * CAP
    return pl.pallas_call(
        _gateup_kernel,
        grid=(N_EXPERTS, FF // tf),
        in_specs=[pl.BlockSpec((1, m, D), lambda e, f: (e, 0, 0)),
                  pl.BlockSpec((1, D, tf), lambda e, f: (e, 0, f)),
                  pl.BlockSpec((1, D, tf), lambda e, f: (e, 0, f))],
        out_specs=pl.BlockSpec((1, m, tf), lambda e, f: (e, 0, f)),
        out_shape=jax.ShapeDtypeStruct((N_EXPERTS, m, FF), BF16),
        compiler_params=_cp(("arbitrary", "arbitrary"), 48),
        name="expert_gate_up",
    )(xe, w_gate, w_up)


def _down_kernel(a_ref, w_ref, o_ref):
    o_ref[0] = _dot(a_ref[0], w_ref[0].astype(BF16)).astype(o_ref.dtype)


def _down(a, w_down):
    tn = 1024
    m = B * CAP
    return pl.pallas_call(
        _down_kernel,
        grid=(N_EXPERTS, D // tn),
        in_specs=[pl.BlockSpec((1, m, FF), lambda e, j: (e, 0, 0)),
                  pl.BlockSpec((1, FF, tn), lambda e, j: (e, 0, j))],
        out_specs=pl.BlockSpec((1, m, tn), lambda e, j: (e, 0, j)),
        out_shape=jax.ShapeDtypeStruct((N_EXPERTS, m, D), BF16),
        compiler_params=_cp(("arbitrary", "arbitrary"), 48),
        name="expert_down",
    )(a, w_down)


def _scatter_kernel(slot_t_ref, w_t_ref, o_ref, x_ref, g_ref, y_ref, pw_ref):
    @pl.when(pl.program_id(1) == 0)
    def _():
        lane = lax.broadcasted_iota(jnp.int32, (S, CAP), 1)
        st = slot_t_ref[0]
        wt = w_t_ref[0]
        for e in range(N_EXPERTS):
            hit = lane == st[:, e:e + 1]
            pw_ref[:, e * CAP:(e + 1) * CAP] = jnp.where(hit, wt[:, e:e + 1], 0.0).astype(BF16)

    o = o_ref[:, 0].reshape(N_EXPERTS * CAP, o_ref.shape[-1])
    y_ref[0] = x_ref[0] + g_ref[0] * _dot(pw_ref[...], o)


def _scatter_residual(slot_t, w_t, out4, x3, gate2):
    tn = 512
    return pl.pallas_call(
        _scatter_kernel,
        grid=(B, D // tn),
        in_specs=[pl.BlockSpec((1, S, N_EXPERTS), lambda b, j: (b, 0, 0)),
                  pl.BlockSpec((1, S, N_EXPERTS), lambda b, j: (b, 0, 0)),
                  pl.BlockSpec((N_EXPERTS, 1, CAP, tn), lambda b, j: (0, b, 0, j)),
                  pl.BlockSpec((1, S, tn), lambda b, j: (b, 0, j)),
                  pl.BlockSpec((1, 1, tn), lambda b, j: (b, 0, j))],
        out_specs=pl.BlockSpec((1, S, tn), lambda b, j: (b, 0, j)),
        out_shape=jax.ShapeDtypeStruct((B, S, D), F32),
        scratch_shapes=[pltpu.VMEM((S, N_EXPERTS * CAP), BF16)],
        compiler_params=_cp(("arbitrary", "arbitrary"), 56),
        name="expert_scatter",
    )(slot_t, w_t, out4, x3, gate2)


def _final_kernel(x_ref, w_ref, o_ref):
    x = x_ref[...]
    ms = jnp.mean(x * x, axis=-1, keepdims=True)
    o_ref[...] = x * lax.rsqrt(ms + EPS) * w_ref[...]


def _final(x2d, w):
    tm = 256
    return pl.pallas_call(
        _final_kernel,
        grid=(T // tm,),
        in_specs=[pl.BlockSpec((tm, D), lambda i: (i, 0)),
                  pl.BlockSpec((1, D), lambda i: (0, 0))],
        out_specs=pl.BlockSpec((tm, D), lambda i: (i, 0)),
        out_shape=jax.ShapeDtypeStruct((T, D), F32),
        compiler_params=_cp(("arbitrary",), 40),
        name="final_norm",
    )(x2d, w)


def _rope_tables():
    half = A_HEAD // 2
    inv_freq = ROPE_THETA ** (-jnp.arange(half, dtype=F32) / half)
    ang = jnp.arange(S, dtype=F32)[:, None] * inv_freq[None, :]
    cos = jnp.cos(ang)
    sin = jnp.sin(ang)
    cos2 = jnp.concatenate([cos, cos], axis=-1)
    sin2 = jnp.concatenate([-sin, sin], axis=-1)
    scale = A_HEAD ** -0.5
    return jnp.stack([cos2 * scale, cos2]), jnp.stack([sin2 * scale, sin2])


def kernel(x, c, norm1_w, norm2_w, normf_w, w_ada, b_ada, w_in, conv_w, conv_b, dt_bias_f, dt_bias_b,
           a_log_f, a_log_b, d_skip, ssm_norm_w, w_ssm_out, w_attn_out, w_o, w_router, w_gate_e, w_up_e,
           w_down_e):
    assert x.shape == (B, S, D) and c.shape == (B, D) and w_in.shape[0] == 1
    x2d = x.reshape(T, D)
    layer = 0

    c8 = jnp.pad(c, ((0, 8 - B), (0, 0)))
    b_ada_l = b_ada[layer][None, :]
    n_early = 2 * D
    shift1, scale1 = [m[:, None, :] for m in jnp.split(_ada(c8, w_ada[layer], b_ada_l, n_early)[:B], 2, axis=-1)]

    dilations = tuple(d for _, d in PATTERNS)
    h1_all = _norm_mod(x2d, norm1_w[layer][None, :], scale1, shift1, dilations[1:])
    h1 = h1_all[0]

    w_t = jnp.swapaxes(w_in[layer], 0, 1)
    zx = _mm(h1, w_t, col0=0, n=OFF_DT, tm=1024, tn=512, out_dtype=BF16, vmem_mb=48, name="proj_z_xbc")
    dt, dt_t = _dt_proj(h1, w_t)
    w_qkvg = _repack_weights(w_t)
    cos_tab, sin_tab = _rope_tables()
    gates, mod_late = _mm_with_modulation(h1, w_qkvg, c8, w_ada[layer], b_ada_l, col0=3 * A_WIDTH, n=2 * D,
                                          mod_col0=n_early, tm=1024, tn=1024)
    gate1, shift2, scale2, gate2 = [m[:, None, :] for m in jnp.split(mod_late[:B], 4, axis=-1)]

    xbc = _conv_silu(zx.reshape(B, S, OFF_DT), conv_w[layer], conv_b[layer][None, :]).reshape(T, CONV_DIM)
    bias = jnp.concatenate([dt_bias_f[layer], dt_bias_b[layer]])
    alog = jnp.concatenate([a_log_f[layer], a_log_b[layer]])
    dsk = jnp.repeat(d_skip[layer], HEAD_P)[None, :]
    expand = (jnp.arange(D_INNER)[None, :] // HEAD_P == jnp.arange(N_HEADS)[:, None]).astype(BF16)
    expand = jnp.concatenate([expand, expand], axis=0)
    yf, yb = _ssd(xbc, dt, dt_t, bias[None, :], bias[:, None], alog[None, :], alog[:, None], dsk, expand)
    yn = _gated_norm(yf, yb, zx, ssm_norm_w[layer][None, :])

    outs, lses = [], []
    for g, dil in enumerate(dilations):
        length = S // dil
        hb, rb = {1: (4, 1), 4: (8, 2), 16: (8, 8)}[dil]
        to_group_order = lambda t: t.reshape(2, length, dil, A_HEAD).transpose(0, 2, 1, 3).reshape(2, S, A_HEAD)
        qkv_g = _qkv_group(h1_all[g].reshape(T, D), w_qkvg, to_group_order(cos_tab), to_group_order(sin_tab), g)
        o_g, lse_g = _attention_group(qkv_g.reshape(B, dil, length, 3 * A_OUT), g, dil, hb, rb)
        outs.append(o_g.transpose(0, 2, 1, 3).reshape(T, A_OUT))
        lses.append(lse_g.transpose(0, 3, 1, 2, 4).reshape(T, A_HPG))
    oc = _combine_groups(outs[0], outs[1], outs[2], jnp.concatenate(lses, axis=-1))

    merged = _merge(yn, oc, w_ssm_out[layer], w_attn_out[layer], gates)
    x1 = _wo_residual(merged, w_o[layer], x2d, gate1)

    wr_pad = jnp.pad(w_router[layer], ((0, 0), (0, LANE - N_EXPERTS)))
    h2, aff = _norm_router(x1, norm2_w[layer][None, :], scale2, shift2, wr_pad)
    aff_t = aff[:, :N_EXPERTS].reshape(B, S, N_EXPERTS).transpose(0, 2, 1)
    upper = (jnp.arange(S)[:, None] < jnp.arange(S)[None, :]).astype(BF16)
    slot, wsel = _topk(aff_t, aff.reshape(B, S, LANE), upper)
    xe = _gather(slot, h2.reshape(B, S, D))
    act = _gateup(xe.reshape(N_EXPERTS, B * CAP, D), w_gate_e[layer], w_up_e[layer])
    out_e = _down(act, w_down_e[layer])
    x2 = _scatter_residual(slot.transpose(0, 2, 1), wsel.transpose(0, 2, 1),
                           out_e.reshape(N_EXPERTS, B, CAP, D), x1.reshape(B, S, D), gate2)

    out = _final(x2.reshape(T, D), normf_w[None, :])
    return out.reshape(B, S, D)
```

```python
import functools
import math

import jax
import jax.numpy as jnp
from jax import lax
from jax.experimental import pallas as pl
from jax.experimental.pallas import tpu as pltpu

F32 = jnp.float32
BF16 = jnp.bfloat16

D = 4096
B = 4
S = 2048
T = B * S
HEAD_P = 64
D_INNER = D // 2
N_HEADS = D_INNER // HEAD_P
HEADS_PER_GROUP = 4
N_GROUPS = N_HEADS // HEADS_PER_GROUP
N_STATE = 128
CONV_K = 5
CHUNK = 128
N_CHUNKS = S // CHUNK
CONV_DIM = D_INNER + 2 * N_GROUPS * N_STATE
PATTERNS = ((128, 1), (512, 4), (2048, 16))
A_HEAD = 128
A_HPG = D // 512
A_WIDTH = len(PATTERNS) * A_HPG * A_HEAD
A_OUT = A_HPG * A_HEAD
ROPE_THETA = 10000.0
NEG_INF = -1e30
N_EXPERTS = 16
FF = D // 2
CAP = 2 * S // N_EXPERTS
EPS = 1e-6
IN_SIZES = (D_INNER, CONV_DIM, 2 * N_HEADS, A_WIDTH, A_WIDTH, A_WIDTH, 2 * D)
OFF_DT = D_INNER + CONV_DIM
OFF_Q = OFF_DT + 2 * N_HEADS
N_QKVG = 3 * A_WIDTH + 2 * D
VG_W = A_WIDTH + 2 * D
LANE = 128


def _cp(sem, vmem_mb=None):
    kw = dict(dimension_semantics=sem)
    if vmem_mb is not None:
        kw["vmem_limit_bytes"] = vmem_mb << 20
    return pltpu.CompilerParams(**kw)


def _dot(a, b):
    return jnp.dot(a, b, preferred_element_type=F32)


def _dot_nt(a, b):
    return lax.dot_general(a, b, (((1,), (1,)), ((), ())), preferred_element_type=F32)


def _split2(v):
    h = v.astype(BF16)
    l = (v - h.astype(F32)).astype(BF16)
    return h, l


def _split3(v):
    h = v.astype(BF16)
    r = v - h.astype(F32)
    m = r.astype(BF16)
    l = (r - m.astype(F32)).astype(BF16)
    return h, m, l


def _silu(v):
    return v * jax.nn.sigmoid(v)


def _softplus(v):
    return jnp.maximum(v, 0.0) + jnp.log1p(jnp.exp(-jnp.abs(v)))


def _ada_kernel(c_ref, w_ref, b_ref, o_ref):
    ca = _silu(c_ref[...]).astype(BF16)
    o_ref[...] = _dot(ca, w_ref[...].astype(BF16)) + b_ref[...]


def _ada(c8, w, b, n):
    tn = 512
    return pl.pallas_call(
        _ada_kernel,
        grid=(n // tn,),
        in_specs=[pl.BlockSpec((8, D), lambda j: (0, 0)),
                  pl.BlockSpec((D, tn), lambda j: (0, j)),
                  pl.BlockSpec((1, tn), lambda j: (0, j))],
        out_specs=pl.BlockSpec((8, tn), lambda j: (0, j)),
        out_shape=jax.ShapeDtypeStruct((8, n), F32),
        compiler_params=_cp(("arbitrary",), 40),
        name="ada_mod",
    )(c8, w, b)


def _norm_mod_kernel(x_ref, w_ref, sc_ref, sh_ref, o_ref, *strided_refs, dilations):
    w = w_ref[...]
    sc = 1.0 + sc_ref[0]
    sh = sh_ref[0]

    x = x_ref[...]
    ms = jnp.mean(x * x, axis=-1, keepdims=True)
    hn = (x * lax.rsqrt(ms + EPS) * w * sc + sh).astype(o_ref.dtype)
    o_ref[...] = hn
    tm = x.shape[0]
    row = lax.broadcasted_iota(jnp.int32, (tm, tm), 0)
    col = lax.broadcasted_iota(jnp.int32, (tm, tm), 1)
    for d, s_ref in zip(dilations, strided_refs):
        per = tm // d
        src = (row & (per - 1)) * d + lax.shift_right_logical(row, per.bit_length() - 1)
        perm = jnp.where(col == src, 1.0, 0.0).astype(BF16)
        res = _dot(perm, hn).astype(o_ref.dtype)
        for r in range(d):
            s_ref[0, r] = res[r * per:(r + 1) * per]


def _norm_mod(x2d, w, sc, sh, dilations):
    tm = 256
    per_b = S // tm
    out_specs = [pl.BlockSpec((tm, D), lambda i: (i, 0))]
    out_shape = [jax.ShapeDtypeStruct((T, D), BF16)]
    for d in dilations:
        out_specs.append(pl.BlockSpec((1, d, tm // d, D), lambda i: (i // per_b, 0, i % per_b, 0)))
        out_shape.append(jax.ShapeDtypeStruct((B, d, S // d, D), BF16))
    return pl.pallas_call(
        functools.partial(_norm_mod_kernel, dilations=dilations),
        grid=(T // tm,),
        in_specs=[pl.BlockSpec((tm, D), lambda i: (i, 0)),
                  pl.BlockSpec((1, D), lambda i: (0, 0)),
                  pl.BlockSpec((1, 1, D), lambda i: (i // per_b, 0, 0)),
                  pl.BlockSpec((1, 1, D), lambda i: (i // per_b, 0, 0))],
        out_specs=out_specs,
        out_shape=out_shape,
        compiler_params=_cp(("arbitrary",), 48),
        name="norm1_mod",
    )(x2d, w, sc, sh)


def _zx_repack_kernel(a_ref, w_ref, r_ref, o_ref, rp_ref, wb_ref):
    @pl.when(pl.program_id(1) == 0)
    def _():
        wb_ref[...] = w_ref[...].T.astype(BF16)

    o_ref[...] = _dot(a_ref[...], wb_ref[...]).astype(o_ref.dtype)
    rp_ref[...] = r_ref[...].T.astype(rp_ref.dtype)


def _proj_zx_and_repack(a, w_t):
    m, k = a.shape
    tm, tn = 1024, 512
    steps_i = m // tm
    rt = 1024
    k_tiles = k // rt
    n_repack = (N_QKVG // rt) * k_tiles
    n_qkv = 3 * A_WIDTH // rt
    n_groups = len(PATTERNS)
    assert n_repack <= (OFF_DT // tn) * steps_i

    def tile(j, i):
        t = jnp.minimum(j * steps_i + i, n_repack - 1)
        return t // k_tiles, t % k_tiles

    def src_map(j, i):
        mt, kk = tile(j, i)
        return (pl.multiple_of(OFF_Q + mt * rt, math.gcd(OFF_Q, rt)), pl.multiple_of(kk * rt, rt))

    def dst_map(j, i):
        mt, kk = tile(j, i)
        return (kk, jnp.where(mt < n_qkv, (mt % n_groups) * 3 + mt // n_groups, mt))

    return pl.pallas_call(
        _zx_repack_kernel,
        grid=(OFF_DT // tn, steps_i),
        in_specs=[pl.BlockSpec((tm, k), lambda j, i: (i, 0)),
                  pl.BlockSpec((tn, k), lambda j, i: (j, 0)),
                  pl.BlockSpec((pl.Element(rt), pl.Element(rt)), src_map)],
        out_specs=[pl.BlockSpec((tm, tn), lambda j, i: (i, j)),
                   pl.BlockSpec((rt, rt), dst_map)],
        out_shape=[jax.ShapeDtypeStruct((m, OFF_DT), BF16),
                   jax.ShapeDtypeStruct((k, N_QKVG), BF16)],
        scratch_shapes=[pltpu.VMEM((k, tn), BF16)],
        compiler_params=_cp(("arbitrary", "arbitrary"), 60),
        name="proj_z_xbc_repack",
    )(a, w_t, w_t)


def _mm_mod_kernel(a_ref, w_ref, c_ref, wa_ref, ba_ref, o_ref, mod_ref):
    o_ref[...] = _dot(a_ref[...], w_ref[...]).astype(o_ref.dtype)
    ca = _silu(c_ref[...]).astype(BF16)
    mod_ref[...] = _dot(ca, wa_ref[...].astype(BF16)) + ba_ref[...]


def _mm_with_modulation(a, w, c8, w_ada, b_ada, *, col0, n, mod_col0, tm, tn):
    m, k = a.shape
    cb = col0 // tn
    steps_i = m // tm
    n_mod = w_ada.shape[1] - mod_col0
    ta = n_mod // ((n // tn) * steps_i)
    assert ta % LANE == 0 and ta * (n // tn) * steps_i == n_mod and mod_col0 % ta == 0
    mb = mod_col0 // ta
    return pl.pallas_call(
        _mm_mod_kernel,
        grid=(n // tn, steps_i),
        in_specs=[pl.BlockSpec((tm, k), lambda j, i: (i, 0)),
                  pl.BlockSpec((k, tn), lambda j, i: (0, cb + j)),
                  pl.BlockSpec((8, D), lambda j, i: (0, 0)),
                  pl.BlockSpec((D, ta), lambda j, i: (0, mb + j * steps_i + i)),
                  pl.BlockSpec((1, ta), lambda j, i: (0, mb + j * steps_i + i))],
        out_specs=[pl.BlockSpec((tm, tn), lambda j, i: (i, j)),
                   pl.BlockSpec((8, ta), lambda j, i: (0, j * steps_i + i))],
        out_shape=[jax.ShapeDtypeStruct((m, n), BF16),
                   jax.ShapeDtypeStruct((8, n_mod), F32)],
        compiler_params=_cp(("arbitrary", "arbitrary"), 60),
        name="proj_gates_mod",
    )(a, w, c8, w_ada, b_ada)


def _qkv_kernel(a_ref, w_ref, cos_ref, sin_ref, o_ref):
    acc = _dot(a_ref[...], w_ref[...])
    j = pl.program_id(0)

    @pl.when(j < 2)
    def _():
        cos = cos_ref[0]
        sin = sin_ref[0]
        for h in range(acc.shape[1] // A_HEAD):
            t = acc[:, h * A_HEAD:(h + 1) * A_HEAD]
            rot = t * cos + pltpu.roll(t, A_HEAD // 2, 1) * sin
            o_ref[:, h * A_HEAD:(h + 1) * A_HEAD] = rot.astype(o_ref.dtype)

    @pl.when(j == 2)
    def _():
        o_ref[...] = acc.astype(o_ref.dtype)


def _qkv_group(a, w_all, cos_tab, sin_tab, g):
    tm, tn = 1024, 1024
    per_b = S // tm
    return pl.pallas_call(
        _qkv_kernel,
        grid=(3, T // tm),
        in_specs=[pl.BlockSpec((tm, D), lambda j, i: (i, 0)),
                  pl.BlockSpec((D, tn), lambda j, i: (0, 3 * g + j)),
                  pl.BlockSpec((1, tm, A_HEAD), lambda j, i: (jnp.minimum(j, 1), i % per_b, 0)),
                  pl.BlockSpec((1, tm, A_HEAD), lambda j, i: (jnp.minimum(j, 1), i % per_b, 0))],
        out_specs=pl.BlockSpec((tm, tn), lambda j, i: (i, j)),
        out_shape=jax.ShapeDtypeStruct((T, 3 * tn), BF16),
        compiler_params=_cp(("arbitrary", "arbitrary"), 48),
        name=f"proj_qkv_g{g}",
    )(a, w_all, cos_tab, sin_tab)


def _dt_kernel(a_ref, w_ref, o_ref, ot_ref):
    ot = _dot_nt(w_ref[...].astype(BF16), a_ref[...])
    ot_ref[...] = ot[0:2 * N_HEADS, :]
    o_ref[...] = ot.T


def _dt_proj(a, w_t):
    tm = 1024
    return pl.pallas_call(
        _dt_kernel,
        grid=(T // tm,),
        in_specs=[pl.BlockSpec((tm, D), lambda i: (i, 0)),
                  pl.BlockSpec((LANE, D), lambda i: (OFF_DT // LANE, 0))],
        out_specs=[pl.BlockSpec((tm, LANE), lambda i: (i, 0)),
                   pl.BlockSpec((2 * N_HEADS, tm), lambda i: (0, i))],
        out_shape=[jax.ShapeDtypeStruct((T, LANE), F32),
                   jax.ShapeDtypeStruct((2 * N_HEADS, T), F32)],
        compiler_params=_cp(("arbitrary",), 40),
        name="proj_dt",
    )(a, w_t)


def _conv_kernel(x_ref, w_ref, b_ref, o_ref):
    x = x_ref[0].astype(F32)
    n = x.shape[0]
    row = lax.broadcasted_iota(jnp.int32, x.shape, 0)
    acc = x * w_ref[CONV_K // 2:CONV_K // 2 + 1, :] + b_ref[...]
    for k in range(CONV_K):
        off = k - CONV_K // 2
        if off == 0:
            continue
        shifted = pltpu.roll(x, (-off) % n, 0)
        ok = (row + off >= 0) & (row + off < n)
        acc = acc + jnp.where(ok, shifted, 0.0) * w_ref[k:k + 1, :]
    o_ref[0] = _silu(acc).astype(o_ref.dtype)


def _conv_silu(zx3, conv_w, conv_b):
    tc = 512
    c0 = D_INNER // tc
    return pl.pallas_call(
        _conv_kernel,
        grid=(B, CONV_DIM // tc),
        in_specs=[pl.BlockSpec((1, S, tc), lambda b, j: (b, 0, c0 + j)),
                  pl.BlockSpec((CONV_K, tc), lambda b, j: (0, j)),
                  pl.BlockSpec((1, tc), lambda b, j: (0, j))],
        out_specs=pl.BlockSpec((1, S, tc), lambda b, j: (b, 0, j)),
        out_shape=jax.ShapeDtypeStruct((B, S, CONV_DIM), BF16),
        compiler_params=_cp(("arbitrary", "arbitrary"), 48),
        name="conv_silu",
    )(zx3, conv_w, conv_b)


def _ssd_direction(x_ref, y_ref, st_ref, dt_raw, dtT_raw, bias, biasT, alog, alogT, e_ref, dsk, reverse):
    tc = CHUNK
    gw = HEADS_PER_GROUP * HEAD_P
    dt = _softplus(dt_raw + bias)
    dtT = _softplus(dtT_raw + biasT)
    da = dt * (-jnp.exp(alog))
    daT = dtT * (-jnp.exp(alogT))
    r = lax.broadcasted_iota(jnp.int32, (tc, tc), 0)
    c = lax.broadcasted_iota(jnp.int32, (tc, tc), 1)
    keep = (c >= r) if reverse else (c <= r)
    tri = jnp.where(keep, 1.0, 0.0).astype(BF16)
    triT = jnp.where((r >= c) if reverse else (r <= c), 1.0, 0.0).astype(BF16)
    h3 = _split3(da)
    cs = _dot(tri, h3[0]) + _dot(tri, h3[1]) + _dot(tri, h3[2])
    t3 = _split3(daT)
    csT = _dot(t3[0], triT) + _dot(t3[1], triT) + _dot(t3[2], triT)
    tot = cs[0:1, :] if reverse else cs[tc - 1:tc, :]
    def hi_lo(v):
        h = v.astype(BF16).astype(F32)
        return jnp.concatenate([h, v - h], axis=1).astype(BF16)

    dt2 = hi_lo(dt)
    ed2 = hi_lo(jnp.exp(tot - cs))
    ec2 = hi_lo(jnp.exp(cs))
    et2 = hi_lo(jnp.broadcast_to(jnp.exp(tot), (16, N_HEADS)))
    lanehead = lax.shift_right_logical(lax.broadcasted_iota(jnp.int32, (tc, gw), 1), HEAD_P.bit_length() - 1)
    for g in range(N_GROUPS):
        eg = e_ref[:, g * gw:(g + 1) * gw]
        dt_e = _dot(dt2, eg)
        edec_e = _dot(ed2, eg)
        ecs_e = _dot(ec2, eg)
        etot_e = _dot(et2, eg)[0:1, :]
        xs = x_ref[:, g * gw:(g + 1) * gw].astype(F32)
        bg = x_ref[:, D_INNER + g * N_STATE:D_INNER + (g + 1) * N_STATE]
        cg = x_ref[:, D_INNER + N_GROUPS * N_STATE + g * N_STATE:D_INNER + N_GROUPS * N_STATE + (g + 1) * N_STATE]
        xd = xs * dt_e
        xdd = (xd * edec_e).astype(BF16)
        cb = _dot_nt(cg, bg)
        ms = []
        rs = []
        for k in range(HEADS_PER_GROUP):
            h = HEADS_PER_GROUP * g + k
            decay = jnp.where(keep, jnp.exp(cs[:, h:h + 1] - csT[h:h + 1, :]), 0.0)
            ms.append((cb * decay).astype(BF16))
            rs.append(jnp.where(lanehead == k, xd, 0.0).astype(BF16))
        lhs = jnp.concatenate(ms, axis=1)
        rhs = jnp.concatenate(rs, axis=0)
        sg = st_ref[g]
        y = _dot(lhs, rhs) + _dot(cg, sg.astype(BF16)) * ecs_e
        if dsk is not None:
            y = y + xs * dsk[:, g * gw:(g + 1) * gw]
        y_ref[:, g * gw:(g + 1) * gw] = y.astype(y_ref.dtype)
        bgT = bg.astype(F32).T.astype(BF16)
        st_ref[g] = sg * etot_e + _dot(bgT, xdd)


def _ssd_kernel(xf_ref, xb_ref, dtf_ref, dtb_ref, dtTf_ref, dtTb_ref, bias_ref, biasT_ref,
                alog_ref, alogT_ref, dsk_ref, e_ref, yf_ref, yb_ref, sf_ref, sb_ref):
    @pl.when(pl.program_id(1) == 0)
    def _():
        sf_ref[...] = jnp.zeros_like(sf_ref)
        sb_ref[...] = jnp.zeros_like(sb_ref)

    nh = N_HEADS
    _ssd_direction(xf_ref, yf_ref, sf_ref, dtf_ref[:, 0:nh], dtTf_ref[0:nh, :], bias_ref[:, 0:nh],
                   biasT_ref[0:nh, :], alog_ref[:, 0:nh], alogT_ref[0:nh, :], e_ref, dsk_ref[...], False)
    _ssd_direction(xb_ref, yb_ref, sb_ref, dtb_ref[:, nh:2 * nh], dtTb_ref[nh:2 * nh, :], bias_ref[:, nh:2 * nh],
                   biasT_ref[nh:2 * nh, :], alog_ref[:, nh:2 * nh], alogT_ref[nh:2 * nh, :], e_ref, None, True)


def _ssd(xbc, dt, dtT, bias, biasT, alog, alogT, dsk, expand):
    nc = N_CHUNKS
    fwd = lambda b, c: (b * nc + c, 0)
    bwd = lambda b, c: (b * nc + nc - 1 - c, 0)
    fwdT = lambda b, c: (0, b * nc + c)
    bwdT = lambda b, c: (0, b * nc + nc - 1 - c)
    const = lambda b, c: (0, 0)
    gw = HEADS_PER_GROUP * HEAD_P
    return pl.pallas_call(
        _ssd_kernel,
        grid=(B, nc),
        in_specs=[pl.BlockSpec((CHUNK, CONV_DIM), fwd),
                  pl.BlockSpec((CHUNK, CONV_DIM), bwd),
                  pl.BlockSpec((CHUNK, LANE), fwd),
                  pl.BlockSpec((CHUNK, LANE), bwd),
                  pl.BlockSpec((2 * N_HEADS, CHUNK), fwdT),
                  pl.BlockSpec((2 * N_HEADS, CHUNK), bwdT),
                  pl.BlockSpec((1, 2 * N_HEADS), const),
                  pl.BlockSpec((2 * N_HEADS, 1), const),
                  pl.BlockSpec((1, 2 * N_HEADS), const),
                  pl.BlockSpec((2 * N_HEADS, 1), const),
                  pl.BlockSpec((1, D_INNER), const),
                  pl.BlockSpec((2 * N_HEADS, D_INNER), const)],
        out_specs=[pl.BlockSpec((CHUNK, D_INNER), fwd),
                   pl.BlockSpec((CHUNK, D_INNER), bwd)],
        out_shape=[jax.ShapeDtypeStruct((T, D_INNER), BF16),
                   jax.ShapeDtypeStruct((T, D_INNER), BF16)],
        scratch_shapes=[pltpu.VMEM((N_GROUPS, N_STATE, gw), F32),
                        pltpu.VMEM((N_GROUPS, N_STATE, gw), F32)],
        compiler_params=_cp(("arbitrary", "arbitrary"), 40),
        name="ssd_scan",
    )(xbc, xbc, dt, dt, dtT, dtT, bias, biasT, alog, alogT, dsk, expand)


def _gnorm_kernel(yf_ref, yb_ref, z_ref, w_ref, o_ref):
    z = z_ref[...].astype(F32)
    y = (yf_ref[...].astype(F32) + yb_ref[...].astype(F32)) * _silu(z)
    ms = jnp.mean(y * y, axis=-1, keepdims=True)
    o_ref[...] = (y * lax.rsqrt(ms + EPS) * w_ref[...]).astype(o_ref.dtype)


def _gated_norm(yf, yb, zx, w):
    tm = 512
    return pl.pallas_call(
        _gnorm_kernel,
        grid=(T // tm,),
        in_specs=[pl.BlockSpec((tm, D_INNER), lambda i: (i, 0)),
                  pl.BlockSpec((tm, D_INNER), lambda i: (i, 0)),
                  pl.BlockSpec((tm, D_INNER), lambda i: (i, 0)),
                  pl.BlockSpec((1, D_INNER), lambda i: (0, 0))],
        out_specs=pl.BlockSpec((tm, D_INNER), lambda i: (i, 0)),
        out_shape=jax.ShapeDtypeStruct((T, D_INNER), BF16),
        compiler_params=_cp(("arbitrary",), 40),
        name="ssm_gated_norm",
    )(yf, yb, zx, w)


def _attn_kernel(q_ref, k_ref, v_ref, o_ref, lse_ref, *, length, hb, rb, radius):
    tq = 128
    win = tq + 2 * radius
    pair = length < win
    assert (not pair) or (2 * length == win and rb % 2 == 0)
    for rr in range(rb):
        for h in range(hb):
            cols = slice(h * A_HEAD, (h + 1) * A_HEAD)
            for qb in range(length // tq):
                q0 = qb * tq
                q = q_ref[0, rr, q0:q0 + tq, cols]
                if pair:
                    r0 = rr - rr % 2
                    ws = -(rr % 2) * length
                    k = k_ref[0, r0:r0 + 2, :, cols].reshape(win, A_HEAD)
                    v = v_ref[0, r0:r0 + 2, :, cols].reshape(win, A_HEAD)
                else:
                    ws = min(max(q0 - radius, 0), length - win)
                    k = k_ref[0, rr, ws:ws + win, cols]
                    v = v_ref[0, rr, ws:ws + win, cols]
                s = _dot_nt(q, k)
                ti = q0 + lax.broadcasted_iota(jnp.int32, (tq, win), 0)
                tj = ws + lax.broadcasted_iota(jnp.int32, (tq, win), 1)
                ok = (tj - ti <= radius) & (ti - tj <= radius)
                if pair:
                    ok = ok & (tj >= 0) & (tj < length)
                s = jnp.where(ok, s, NEG_INF)
                m = jnp.max(s, axis=-1, keepdims=True)
                p = jnp.exp(s - m)
                l = jnp.sum(p, axis=-1, keepdims=True)
                o = _dot(p.astype(BF16), v) / l
                o_ref[0, rr, q0:q0 + tq, cols] = o.astype(o_ref.dtype)
                lse_ref[0, rr, 0, q0:q0 + tq, h:h + 1] = m + jnp.log(l)


def _attention_group(qkv4, g, dil, hb, rb):
    length = S // dil
    window = PATTERNS[g][0]
    radius = window // (2 * dil)
    bw = hb * A_HEAD
    nj = A_HPG // hb
    return pl.pallas_call(
        functools.partial(_attn_kernel, length=length, hb=hb, rb=rb, radius=radius),
        grid=(B, dil // rb, nj),
        in_specs=[pl.BlockSpec((1, rb, length, bw), lambda b, r, j: (b, r, 0, j)),
                  pl.BlockSpec((1, rb, length, bw), lambda b, r, j: (b, r, 0, nj + j)),
                  pl.BlockSpec((1, rb, length, bw), lambda b, r, j: (b, r, 0, 2 * nj + j))],
        out_specs=[pl.BlockSpec((1, rb, length, bw), lambda b, r, j: (b, r, 0, j)),
                   pl.BlockSpec((1, rb, 1, length, hb), lambda b, r, j: (b, r, j, 0, 0))],
        out_shape=[jax.ShapeDtypeStruct((B, dil, length, A_OUT), BF16),
                   jax.ShapeDtypeStruct((B, dil, nj, length, hb), F32)],
        compiler_params=_cp(("arbitrary", "arbitrary", "arbitrary"), 40),
        name=f"attn_dil{dil}",
    )(qkv4, qkv4, qkv4)


def _combine_kernel(o0_ref, o1_ref, o2_ref, lse_ref, out_ref):
    lse = lse_ref[...]
    for hh in range(A_HPG):
        l0 = lse[:, hh:hh + 1]
        l1 = lse[:, A_HPG + hh:A_HPG + hh + 1]
        l2 = lse[:, 2 * A_HPG + hh:2 * A_HPG + hh + 1]
        m = jnp.maximum(jnp.maximum(l0, l1), l2)
        e0 = jnp.exp(l0 - m)
        e1 = jnp.exp(l1 - m)
        e2 = jnp.exp(l2 - m)
        inv = 1.0 / (e0 + e1 + e2)
        cols = slice(hh * A_HEAD, (hh + 1) * A_HEAD)
        mix = ((e0 * inv) * o0_ref[:, cols].astype(F32) + (e1 * inv) * o1_ref[:, cols].astype(F32)
               + (e2 * inv) * o2_ref[:, cols].astype(F32))
        out_ref[:, cols] = mix.astype(out_ref.dtype)


def _combine_groups(o0, o1, o2, lse):
    tm = 512
    n_l = lse.shape[1]
    return pl.pallas_call(
        _combine_kernel,
        grid=(T // tm,),
        in_specs=[pl.BlockSpec((tm, A_OUT), lambda i: (i, 0)),
                  pl.BlockSpec((tm, A_OUT), lambda i: (i, 0)),
                  pl.BlockSpec((tm, A_OUT), lambda i: (i, 0)),
                  pl.BlockSpec((tm, n_l), lambda i: (i, 0))],
        out_specs=pl.BlockSpec((tm, A_OUT), lambda i: (i, 0)),
        out_shape=jax.ShapeDtypeStruct((T, A_OUT), BF16),
        compiler_params=_cp(("arbitrary",), 40),
        name="attn_combine",
    )(o0, o1, o2, lse)


def _merge_kernel(ys_ref, oa_ref, ws_ref, wa_ref, gs_ref, ga_ref, o_ref, wsb_ref, wab_ref):
    @pl.when(pl.program_id(1) == 0)
    def _():
        wsb_ref[...] = ws_ref[...].astype(BF16)
        wab_ref[...] = wa_ref[...].astype(BF16)

    y_ssm = _dot(ys_ref[...], wsb_ref[...])
    y_att = _dot(oa_ref[...], wab_ref[...])
    g_s = jax.nn.sigmoid(gs_ref[...].astype(F32))
    g_a = jax.nn.sigmoid(ga_ref[...].astype(F32))
    o_ref[...] = (g_s * y_ssm + g_a * y_att).astype(o_ref.dtype)


def _merge(yn, oc, w_ssm_out, w_attn_out, gates):
    tm, tn = 1024, 512
    g0 = 0
    g1 = D // tn
    return pl.pallas_call(
        _merge_kernel,
        grid=(D // tn, T // tm),
        in_specs=[pl.BlockSpec((tm, D_INNER), lambda j, i: (i, 0)),
                  pl.BlockSpec((tm, A_OUT), lambda j, i: (i, 0)),
                  pl.BlockSpec((D_INNER, tn), lambda j, i: (0, j)),
                  pl.BlockSpec((A_OUT, tn), lambda j, i: (0, j)),
                  pl.BlockSpec((tm, tn), lambda j, i: (i, g0 + j)),
                  pl.BlockSpec((tm, tn), lambda j, i: (i, g1 + j))],
        out_specs=pl.BlockSpec((tm, tn), lambda j, i: (i, j)),
        out_shape=jax.ShapeDtypeStruct((T, D), BF16),
        scratch_shapes=[pltpu.VMEM((D_INNER, tn), BF16), pltpu.VMEM((A_OUT, tn), BF16)],
        compiler_params=_cp(("arbitrary", "arbitrary"), 48),
        name="branch_merge",
    )(yn, oc, w_ssm_out, w_attn_out, gates, gates)


def _wo_kernel(a_ref, w_ref, x_ref, g_ref, o_ref, wb_ref):
    @pl.when(pl.program_id(1) == 0)
    def _():
        wb_ref[...] = w_ref[...].astype(BF16)

    o_ref[...] = x_ref[...] + g_ref[0] * _dot(a_ref[...], wb_ref[...])


def _wo_residual(merged, w_o, x2d, gate1):
    tm, tn = 1024, 512
    per_b = S // tm
    return pl.pallas_call(
        _wo_kernel,
        grid=(D // tn, T // tm),
        in_specs=[pl.BlockSpec((tm, D), lambda j, i: (i, 0)),
                  pl.BlockSpec((D, tn), lambda j, i: (0, j)),
                  pl.BlockSpec((tm, tn), lambda j, i: (i, j)),
                  pl.BlockSpec((1, 1, tn), lambda j, i: (i // per_b, 0, j))],
        out_specs=pl.BlockSpec((tm, tn), lambda j, i: (i, j)),
        out_shape=jax.ShapeDtypeStruct((T, D), F32),
        scratch_shapes=[pltpu.VMEM((D, tn), BF16)],
        compiler_params=_cp(("arbitrary", "arbitrary"), 52),
        name="out_proj_residual",
    )(merged, w_o, x2d, gate1)


def _norm_router_kernel(x_ref, w_ref, sc_ref, sh_ref, wr_ref, h_ref, aff_ref):
    x = x_ref[...]
    ms = jnp.mean(x * x, axis=-1, keepdims=True)
    h = x * lax.rsqrt(ms + EPS) * w_ref[...] * (1.0 + sc_ref[0]) + sh_ref[0]
    h_ref[...] = h.astype(h_ref.dtype)
    hh, hl = _split2(h)
    wh, wl = _split2(wr_ref[...])
    logits = _dot(hh, wh) + _dot(hl, wh) + _dot(hh, wl)
    lane = lax.broadcasted_iota(jnp.int32, logits.shape, 1)
    logits = jnp.where(lane < N_EXPERTS, logits, NEG_INF)
    m = jnp.max(logits, axis=-1, keepdims=True)
    e = jnp.exp(logits - m)
    aff_ref[...] = e / jnp.sum(e, axis=-1, keepdims=True)


def _norm_router(x2d, w, sc, sh, wr_pad):
    tm = 256
    per_b = S // tm
    return pl.pallas_call(
        _norm_router_kernel,
        grid=(T // tm,),
        in_specs=[pl.BlockSpec((tm, D), lambda i: (i, 0)),
                  pl.BlockSpec((1, D), lambda i: (0, 0)),
                  pl.BlockSpec((1, 1, D), lambda i: (i // per_b, 0, 0)),
                  pl.BlockSpec((1, 1, D), lambda i: (i // per_b, 0, 0)),
                  pl.BlockSpec((D, LANE), lambda i: (0, 0))],
        out_specs=[pl.BlockSpec((tm, D), lambda i: (i, 0)),
                   pl.BlockSpec((tm, LANE), lambda i: (i, 0))],
        out_shape=[jax.ShapeDtypeStruct((T, D), BF16),
                   jax.ShapeDtypeStruct((T, LANE), F32)],
        compiler_params=_cp(("arbitrary",), 40),
        name="norm2_router",
    )(x2d, w, sc, sh, wr_pad)


def _topk_kernel(a_ref, at_ref, u_ref, slot_ref, w_ref, rank_ref):
    rb = 256
    a_tok = at_ref[0]
    lane = lax.broadcasted_iota(jnp.int32, a_tok.shape, 1)
    ones = jnp.ones((16, rb), BF16)

    def count_greater(e, carry):
        a_row = a_ref[0, pl.ds(e, 1), :]
        a_col = jnp.sum(jnp.where(lane == e, a_tok, 0.0), axis=1, keepdims=True)
        r = jnp.zeros((16, S), F32)
        for blk in range(S // rb):
            greater = jnp.where(a_col[blk * rb:(blk + 1) * rb] > a_row, 1.0, 0.0).astype(BF16)
            r = r + _dot(ones, greater)
        rank_ref[pl.ds(e, 1), :] = r[0:1]
        return carry

    lax.fori_loop(0, N_EXPERTS, count_greater, 0)
    a = a_ref[0]
    in_top = rank_ref[...] < CAP
    n_top = jnp.sum(jnp.where(in_top, 1.0, 0.0), axis=1, keepdims=True)
    v = jnp.min(jnp.where(in_top, a, jnp.inf), axis=1, keepdims=True)
    tie = a == v
    n_tie = jnp.sum(jnp.where(tie, 1.0, 0.0), axis=1, keepdims=True)
    need = CAP - (n_top - n_tie)
    u = u_ref[...]
    tie_rank = _dot(jnp.where(tie, 1.0, 0.0).astype(BF16), u)
    sel = (in_top & (a > v)) | (tie & (tie_rank < need))
    slot = _dot(jnp.where(sel, 1.0, 0.0).astype(BF16), u)
    slot_ref[0] = jnp.where(sel, slot.astype(jnp.int32), -1)
    w_ref[0] = jnp.where(sel, a, 0.0)


def _topk(aff_t, aff3, upper):
    return pl.pallas_call(
        _topk_kernel,
        grid=(B,),
        in_specs=[pl.BlockSpec((1, N_EXPERTS, S), lambda b: (b, 0, 0)),
                  pl.BlockSpec((1, S, LANE), lambda b: (b, 0, 0)),
                  pl.BlockSpec((S, S), lambda b: (0, 0))],
        out_specs=[pl.BlockSpec((1, N_EXPERTS, S), lambda b: (b, 0, 0)),
                   pl.BlockSpec((1, N_EXPERTS, S), lambda b: (b, 0, 0))],
        out_shape=[jax.ShapeDtypeStruct((B, N_EXPERTS, S), jnp.int32),
                   jax.ShapeDtypeStruct((B, N_EXPERTS, S), F32)],
        scratch_shapes=[pltpu.VMEM((N_EXPERTS, S), F32)],
        compiler_params=_cp(("arbitrary",), 48),
        name="expert_topk",
    )(aff_t, aff3, upper)


def _gather_kernel(slot_ref, h_ref, xe_ref):
    hblk = h_ref[0]
    j = lax.broadcasted_iota(jnp.int32, (CAP, S), 0)
    for e in range(N_EXPERTS):
        p = jnp.where(j == slot_ref[0, e:e + 1, :], 1.0, 0.0).astype(BF16)
        xe_ref[e, 0] = _dot(p, hblk).astype(xe_ref.dtype)


def _gather(slot, h3):
    td = 1024
    return pl.pallas_call(
        _gather_kernel,
        grid=(B, D // td),
        in_specs=[pl.BlockSpec((1, N_EXPERTS, S), lambda b, j: (b, 0, 0)),
                  pl.BlockSpec((1, S, td), lambda b, j: (b, 0, j))],
        out_specs=pl.BlockSpec((N_EXPERTS, 1, CAP, td), lambda b, j: (0, b, 0, j)),
        out_shape=jax.ShapeDtypeStruct((N_EXPERTS, B, CAP, D), BF16),
        compiler_params=_cp(("arbitrary", "arbitrary"), 48),
        name="expert_gather",
    )(slot, h3)


def _gateup_kernel(x_ref, wg_ref, wu_ref, a_ref):
    x = x_ref[0]
    g = _dot(x, wg_ref[0].astype(BF16))
    u = _dot(x, wu_ref[0].astype(BF16))
    a_ref[0] = (_silu(g) * u).astype(a_ref.dtype)


def _gateup(xe, w_gate, w_up):
    tf = 256
    m = B * CAP
    return pl.pallas_call(
        _gateup_kernel,
        grid=(N_EXPERTS, FF // tf),
        in_specs=[pl.BlockSpec((1, m, D), lambda e, f: (e, 0, 0)),
                  pl.BlockSpec((1, D, tf), lambda e, f: (e, 0, f)),
                  pl.BlockSpec((1, D, tf), lambda e, f: (e, 0, f))],
        out_specs=pl.BlockSpec((1, m, tf), lambda e, f: (e, 0, f)),
        out_shape=jax.ShapeDtypeStruct((N_EXPERTS, m, FF), BF16),
        compiler_params=_cp(("arbitrary", "arbitrary"), 48),
        name="expert_gate_up",
    )(xe, w_gate, w_up)


def _down_kernel(a_ref, w_ref, o_ref):
    o_ref[0] = _dot(a_ref[0], w_ref[0].astype(BF16)).astype(o_ref.dtype)


def _down(a, w_down):
    tn = 1024
    m = B * CAP
    return pl.pallas_call(
        _down_kernel,
        grid=(N_EXPERTS, D // tn),
        in_specs=[pl.BlockSpec((1, m, FF), lambda e, j: (e, 0, 0)),
                  pl.BlockSpec((1, FF, tn), lambda e, j: (e, 0, j))],
        out_specs=pl.BlockSpec((1, m, tn), lambda e, j: (e, 0, j)),
        out_shape=jax.ShapeDtypeStruct((N_EXPERTS, m, D), BF16),
        compiler_params=_cp(("arbitrary", "arbitrary"), 48),
        name="expert_down",
    )(a, w_down)


def _scatter_kernel(slot_t_ref, w_t_ref, o_ref, x_ref, g_ref, y_ref, pw_ref):
    @pl.when(pl.program_id(1) == 0)
    def _():
        lane = lax.broadcasted_iota(jnp.int32, (S, CAP), 1)
        st = slot_t_ref[0]
        wt = w_t_ref[0]
        for e in range(N_EXPERTS):
            hit = lane == st[:, e:e + 1]
            pw_ref[:, e * CAP:(e + 1) * CAP] = jnp.where(hit, wt[:, e:e + 1], 0.0).astype(BF16)

    o = o_ref[:, 0].reshape(N_EXPERTS * CAP, o_ref.shape[-1])
    y_ref[0] = x_ref[0] + g_ref[0] * _dot(pw_ref[...], o)


def _scatter_residual(slot_t, w_t, out4, x3, gate2):
    tn = 512
    return pl.pallas_call(
        _scatter_kernel,
        grid=(B, D // tn),
        in_specs=[pl.BlockSpec((1, S, N_EXPERTS), lambda b, j: (b, 0, 0)),
                  pl.BlockSpec((1, S, N_EXPERTS), lambda b, j: (b, 0, 0)),
                  pl.BlockSpec((N_EXPERTS, 1, CAP, tn), lambda b, j: (0, b, 0, j)),
                  pl.BlockSpec((1, S, tn), lambda b, j: (b, 0, j)),
                  pl.BlockSpec((1, 1, tn), lambda b, j: (b, 0, j))],
        out_specs=pl.BlockSpec((1, S, tn), lambda b, j: (b, 0, j)),
        out_shape=jax.ShapeDtypeStruct((B, S, D), F32),
        scratch_shapes=[pltpu.VMEM((S, N_EXPERTS * CAP), BF16)],
        compiler_params=_cp(("arbitrary", "arbitrary"), 56),
        name="expert_scatter",
    )(slot_t, w_t, out4, x3, gate2)


def _final_kernel(x_ref, w_ref, o_ref):
    x = x_ref[...]
    ms = jnp.mean(x * x, axis=-1, keepdims=True)
    o_ref[...] = x * lax.rsqrt(ms + EPS) * w_ref[...]


def _final(x2d, w):
    tm = 256
    return pl.pallas_call(
        _final_kernel,
        grid=(T // tm,),
        in_specs=[pl.BlockSpec((tm, D), lambda i: (i, 0)),
                  pl.BlockSpec((1, D), lambda i: (0, 0))],
        out_specs=pl.BlockSpec((tm, D), lambda i: (i, 0)),
        out_shape=jax.ShapeDtypeStruct((T, D), F32),
        compiler_params=_cp(("arbitrary",), 40),
        name="final_norm",
    )(x2d, w)


def _rope_tables():
    half = A_HEAD // 2
    inv_freq = ROPE_THETA ** (-jnp.arange(half, dtype=F32) / half)
    ang = jnp.arange(S, dtype=F32)[:, None] * inv_freq[None, :]
    cos = jnp.cos(ang)
    sin = jnp.sin(ang)
    cos2 = jnp.concatenate([cos, cos], axis=-1)
    sin2 = jnp.concatenate([-sin, sin], axis=-1)
    scale = A_HEAD ** -0.5
    return jnp.stack([cos2 * scale, cos2]), jnp.stack([sin2 * scale, sin2])


def kernel(x, c, norm1_w, norm2_w, normf_w, w_ada, b_ada, w_in, conv_w, conv_b, dt_bias_f, dt_bias_b,
           a_log_f, a_log_b, d_skip, ssm_norm_w, w_ssm_out, w_attn_out, w_o, w_router, w_gate_e, w_up_e,
           w_down_e):
    assert x.shape == (B, S, D) and c.shape == (B, D) and w_in.shape[0] == 1
    x2d = x.reshape(T, D)
    layer = 0

    c8 = jnp.pad(c, ((0, 8 - B), (0, 0)))
    b_ada_l = b_ada[layer][None, :]
    n_early = 2 * D
    shift1, scale1 = [m[:, None, :] for m in jnp.split(_ada(c8, w_ada[layer], b_ada_l, n_early)[:B], 2, axis=-1)]

    dilations = tuple(d for _, d in PATTERNS)
    h1_all = _norm_mod(x2d, norm1_w[layer][None, :], scale1, shift1, dilations[1:])
    h1 = h1_all[0]

    w_t = jnp.swapaxes(w_in[layer], 0, 1)
    zx, w_qkvg = _proj_zx_and_repack(h1, w_t)
    dt, dt_t = _dt_proj(h1, w_t)
    cos_tab, sin_tab = _rope_tables()
    gates, mod_late = _mm_with_modulation(h1, w_qkvg, c8, w_ada[layer], b_ada_l, col0=3 * A_WIDTH, n=2 * D,
                                          mod_col0=n_early, tm=1024, tn=1024)
    gate1, shift2, scale2, gate2 = [m[:, None, :] for m in jnp.split(mod_late[:B], 4, axis=-1)]

    xbc = _conv_silu(zx.reshape(B, S, OFF_DT), conv_w[layer], conv_b[layer][None, :]).reshape(T, CONV_DIM)
    bias = jnp.concatenate([dt_bias_f[layer], dt_bias_b[layer]])
    alog = jnp.concatenate([a_log_f[layer], a_log_b[layer]])
    dsk = jnp.repeat(d_skip[layer], HEAD_P)[None, :]
    expand = (jnp.arange(D_INNER)[None, :] // HEAD_P == jnp.arange(N_HEADS)[:, None]).astype(BF16)
    expand = jnp.concatenate([expand, expand], axis=0)
    yf, yb = _ssd(xbc, dt, dt_t, bias[None, :], bias[:, None], alog[None, :], alog[:, None], dsk, expand)
    yn = _gated_norm(yf, yb, zx, ssm_norm_w[layer][None, :])

    outs, lses = [], []
    for g, dil in enumerate(dilations):
        length = S // dil
        hb, rb = {1: (4, 1), 4: (8, 2), 16: (8, 8)}[dil]
        to_group_order = lambda t: t.reshape(2, length, dil, A_HEAD).transpose(0, 2, 1, 3).reshape(2, S, A_HEAD)
        qkv_g = _qkv_group(h1_all[g].reshape(T, D), w_qkvg, to_group_order(cos_tab), to_group_order(sin_tab), g)
        o_g, lse_g = _attention_group(qkv_g.reshape(B, dil, length, 3 * A_OUT), g, dil, hb, rb)
        outs.append(o_g.transpose(0, 2, 1, 3).reshape(T, A_OUT))
        lses.append(lse_g.transpose(0, 3, 1, 2, 4).reshape(T, A_HPG))
    oc = _combine_groups(outs[0], outs[1], outs[2], jnp.concatenate(lses, axis=-1))

    merged = _merge(yn, oc, w_ssm_out[layer], w_attn_out[layer], gates)
    x1 = _wo_residual(merged, w_o[layer], x2d, gate1)

    wr_pad = jnp.pad(w_router[layer], ((0, 0), (0, LANE - N_EXPERTS)))
    h2, aff = _norm_router(x1, norm2_w[layer][None, :], scale2, shift2, wr_pad)
    aff_t = aff[:, :N_EXPERTS].reshape(B, S, N_EXPERTS).transpose(0, 2, 1)
    upper = (jnp.arange(S)[:, None] < jnp.arange(S)[None, :]).astype(BF16)
    slot, wsel = _topk(aff_t, aff.reshape(B, S, LANE), upper)
    xe = _gather(slot, h2.reshape(B, S, D))
    act = _gateup(xe.reshape(N_EXPERTS, B * CAP, D), w_gate_e[layer], w_up_e[layer])
    out_e = _down(act, w_down_e[layer])
    x2 = _scatter_residual(slot.transpose(0, 2, 1), wsel.transpose(0, 2, 1),
                           out_e.reshape(N_EXPERTS, B, CAP, D), x1.reshape(B, S, D), gate2)

    out = _final(x2.reshape(T, D), normf_w[None, :])
    return out.reshape(B, S, D)
```

```python
import functools
import math

import jax
import jax.numpy as jnp
from jax import lax
from jax.experimental import pallas as pl
from jax.experimental.pallas import tpu as pltpu

F32 = jnp.float32
BF16 = jnp.bfloat16

D = 4096
B = 4
S = 2048
T = B * S
HEAD_P = 64
D_INNER = D // 2
N_HEADS = D_INNER // HEAD_P
HEADS_PER_GROUP = 4
N_GROUPS = N_HEADS // HEADS_PER_GROUP
N_STATE = 128
CONV_K = 5
CHUNK = 128
N_CHUNKS = S // CHUNK
CONV_DIM = D_INNER + 2 * N_GROUPS * N_STATE
PATTERNS = ((128, 1), (512, 4), (2048, 16))
A_HEAD = 128
A_HPG = D // 512
A_WIDTH = len(PATTERNS) * A_HPG * A_HEAD
A_OUT = A_HPG * A_HEAD
ROPE_THETA = 10000.0
NEG_INF = -1e30
N_EXPERTS = 16
FF = D // 2
CAP = 2 * S // N_EXPERTS
EPS = 1e-6
IN_SIZES = (D_INNER, CONV_DIM, 2 * N_HEADS, A_WIDTH, A_WIDTH, A_WIDTH, 2 * D)
OFF_DT = D_INNER + CONV_DIM
OFF_Q = OFF_DT + 2 * N_HEADS
N_QKVG = 3 * A_WIDTH + 2 * D
VG_W = A_WIDTH + 2 * D
LANE = 128


def _cp(sem, vmem_mb=None):
    kw = dict(dimension_semantics=sem)
    if vmem_mb is not None:
        kw["vmem_limit_bytes"] = vmem_mb << 20
    return pltpu.CompilerParams(**kw)


def _dot(a, b):
    return jnp.dot(a, b, preferred_element_type=F32)


def _dot_nt(a, b):
    return lax.dot_general(a, b, (((1,), (1,)), ((), ())), preferred_element_type=F32)


def _split2(v):
    h = v.astype(BF16)
    l = (v - h.astype(F32)).astype(BF16)
    return h, l


def _split3(v):
    h = v.astype(BF16)
    r = v - h.astype(F32)
    m = r.astype(BF16)
    l = (r - m.astype(F32)).astype(BF16)
    return h, m, l


def _silu(v):
    return v * jax.nn.sigmoid(v)


def _softplus(v):
    return jnp.maximum(v, 0.0) + jnp.log1p(jnp.exp(-jnp.abs(v)))


def _ada_kernel(c_ref, w_ref, b_ref, o_ref):
    ca = _silu(c_ref[...]).astype(BF16)
    o_ref[...] = _dot(ca, w_ref[...].astype(BF16)) + b_ref[...]


def _ada(c8, w, b, n):
    tn = 512
    return pl.pallas_call(
        _ada_kernel,
        grid=(n // tn,),
        in_specs=[pl.BlockSpec((8, D), lambda j: (0, 0)),
                  pl.BlockSpec((D, tn), lambda j: (0, j)),
                  pl.BlockSpec((1, tn), lambda j: (0, j))],
        out_specs=pl.BlockSpec((8, tn), lambda j: (0, j)),
        out_shape=jax.ShapeDtypeStruct((8, n), F32),
        compiler_params=_cp(("arbitrary",), 40),
        name="ada_mod",
    )(c8, w, b)


def _norm_mod_kernel(x_ref, w_ref, sc_ref, sh_ref, o_ref, *strided_refs, dilations):
    w = w_ref[...]
    sc = 1.0 + sc_ref[0]
    sh = sh_ref[0]

    x = x_ref[...]
    ms = jnp.mean(x * x, axis=-1, keepdims=True)
    hn = (x * lax.rsqrt(ms + EPS) * w * sc + sh).astype(o_ref.dtype)
    o_ref[...] = hn
    tm = x.shape[0]
    row = lax.broadcasted_iota(jnp.int32, (tm, tm), 0)
    col = lax.broadcasted_iota(jnp.int32, (tm, tm), 1)
    for d, s_ref in zip(dilations, strided_refs):
        per = tm // d
        src = (row & (per - 1)) * d + lax.shift_right_logical(row, per.bit_length() - 1)
        perm = jnp.where(col == src, 1.0, 0.0).astype(BF16)
        res = _dot(perm, hn).astype(o_ref.dtype)
        for r in range(d):
            s_ref[0, r] = res[r * per:(r + 1) * per]


def _norm_mod(x2d, w, sc, sh, dilations):
    tm = 256
    per_b = S // tm
    out_specs = [pl.BlockSpec((tm, D), lambda i: (i, 0))]
    out_shape = [jax.ShapeDtypeStruct((T, D), BF16)]
    for d in dilations:
        out_specs.append(pl.BlockSpec((1, d, tm // d, D), lambda i: (i // per_b, 0, i % per_b, 0)))
        out_shape.append(jax.ShapeDtypeStruct((B, d, S // d, D), BF16))
    return pl.pallas_call(
        functools.partial(_norm_mod_kernel, dilations=dilations),
        grid=(T // tm,),
        in_specs=[pl.BlockSpec((tm, D), lambda i: (i, 0)),
                  pl.BlockSpec((1, D), lambda i: (0, 0)),
                  pl.BlockSpec((1, 1, D), lambda i: (i // per_b, 0, 0)),
                  pl.BlockSpec((1, 1, D), lambda i: (i // per_b, 0, 0))],
        out_specs=out_specs,
        out_shape=out_shape,
        compiler_params=_cp(("arbitrary",), 48),
        name="norm1_mod",
    )(x2d, w, sc, sh)


def _zx_repack_kernel(a_ref, w_ref, r_ref, o_ref, rp_ref, wb_ref):
    @pl.when(pl.program_id(1) == 0)
    def _():
        wb_ref[...] = w_ref[...].T.astype(BF16)

    o_ref[...] = _dot(a_ref[...], wb_ref[...]).astype(o_ref.dtype)
    rp_ref[...] = r_ref[...].T.astype(rp_ref.dtype)


def _proj_zx_and_repack(a, w_t):
    m, k = a.shape
    tm, tn = 1024, 512
    steps_i = m // tm
    rt = 1024
    k_tiles = k // rt
    n_repack = (N_QKVG // rt) * k_tiles
    n_qkv = 3 * A_WIDTH // rt
    n_groups = len(PATTERNS)
    assert n_repack <= (OFF_DT // tn) * steps_i

    def tile(j, i):
        t = jnp.minimum(j * steps_i + i, n_repack - 1)
        return t // k_tiles, t % k_tiles

    def src_map(j, i):
        mt, kk = tile(j, i)
        return (pl.multiple_of(OFF_Q + mt * rt, math.gcd(OFF_Q, rt)), pl.multiple_of(kk * rt, rt))

    def dst_map(j, i):
        mt, kk = tile(j, i)
        return (kk, jnp.where(mt < n_qkv, (mt % n_groups) * 3 + mt // n_groups, mt))

    return pl.pallas_call(
        _zx_repack_kernel,
        grid=(OFF_DT // tn, steps_i),
        in_specs=[pl.BlockSpec((tm, k), lambda j, i: (i, 0)),
                  pl.BlockSpec((tn, k), lambda j, i: (j, 0)),
                  pl.BlockSpec((pl.Element(rt), pl.Element(rt)), src_map)],
        out_specs=[pl.BlockSpec((tm, tn), lambda j, i: (i, j)),
                   pl.BlockSpec((rt, rt), dst_map)],
        out_shape=[jax.ShapeDtypeStruct((m, OFF_DT), BF16),
                   jax.ShapeDtypeStruct((k, N_QKVG), BF16)],
        scratch_shapes=[pltpu.VMEM((k, tn), BF16)],
        compiler_params=_cp(("arbitrary", "arbitrary"), 60),
        name="proj_z_xbc_repack",
    )(a, w_t, w_t)


def _mm_mod_kernel(a_ref, w_ref, c_ref, wa_ref, ba_ref, o_ref, mod_ref):
    o_ref[...] = _dot(a_ref[...], w_ref[...]).astype(o_ref.dtype)
    ca = _silu(c_ref[...]).astype(BF16)
    mod_ref[...] = _dot(ca, wa_ref[...].astype(BF16)) + ba_ref[...]


def _mm_with_modulation(a, w, c8, w_ada, b_ada, *, col0, n, mod_col0, tm, tn):
    m, k = a.shape
    cb = col0 // tn
    steps_i = m // tm
    n_mod = w_ada.shape[1] - mod_col0
    ta = n_mod // ((n // tn) * steps_i)
    assert ta % LANE == 0 and ta * (n // tn) * steps_i == n_mod and mod_col0 % ta == 0
    mb = mod_col0 // ta
    return pl.pallas_call(
        _mm_mod_kernel,
        grid=(n // tn, steps_i),
        in_specs=[pl.BlockSpec((tm, k), lambda j, i: (i, 0)),
                  pl.BlockSpec((k, tn), lambda j, i: (0, cb + j)),
                  pl.BlockSpec((8, D), lambda j, i: (0, 0)),
                  pl.BlockSpec((D, ta), lambda j, i: (0, mb + j * steps_i + i)),
                  pl.BlockSpec((1, ta), lambda j, i: (0, mb + j * steps_i + i))],
        out_specs=[pl.BlockSpec((tm, tn), lambda j, i: (i, j)),
                   pl.BlockSpec((8, ta), lambda j, i: (0, j * steps_i + i))],
        out_shape=[jax.ShapeDtypeStruct((m, n), BF16),
                   jax.ShapeDtypeStruct((8, n_mod), F32)],
        compiler_params=_cp(("arbitrary", "arbitrary"), 60),
        name="proj_gates_mod",
    )(a, w, c8, w_ada, b_ada)


def _qkv_kernel(a_ref, w_ref, cos_ref, sin_ref, o_ref):
    cos = cos_ref[0]
    sin = sin_ref[0]
    half = w_ref.shape[1] // 2
    for part in range(2):
        acc = _dot(a_ref[...], w_ref[:, part * half:(part + 1) * half])
        for h in range(half // A_HEAD):
            t = acc[:, h * A_HEAD:(h + 1) * A_HEAD]
            rot = t * cos + pltpu.roll(t, A_HEAD // 2, 1) * sin
            c0 = part * half + h * A_HEAD
            o_ref[:, c0:c0 + A_HEAD] = rot.astype(o_ref.dtype)


def _qkv_group(a, w_all, cos_tab, sin_tab, g):
    tm, tn = 1024, 1024
    per_b = S // tm
    return pl.pallas_call(
        _qkv_kernel,
        grid=(3, T // tm),
        in_specs=[pl.BlockSpec((tm, D), lambda j, i: (i, 0)),
                  pl.BlockSpec((D, tn), lambda j, i: (0, 3 * g + j)),
                  pl.BlockSpec((1, tm, A_HEAD), lambda j, i: (j, i % per_b, 0)),
                  pl.BlockSpec((1, tm, A_HEAD), lambda j, i: (j, i % per_b, 0))],
        out_specs=pl.BlockSpec((tm, tn), lambda j, i: (i, j)),
        out_shape=jax.ShapeDtypeStruct((T, 3 * tn), BF16),
        compiler_params=_cp(("arbitrary", "arbitrary"), 48),
        name=f"proj_qkv_g{g}",
    )(a, w_all, cos_tab, sin_tab)


def _dt_kernel(a_ref, w_ref, o_ref, ot_ref):
    ot = _dot_nt(w_ref[...].astype(BF16), a_ref[...])
    ot_ref[...] = ot[0:2 * N_HEADS, :]
    o_ref[...] = ot.T


def _dt_proj(a, w_t):
    tm = 1024
    return pl.pallas_call(
        _dt_kernel,
        grid=(T // tm,),
        in_specs=[pl.BlockSpec((tm, D), lambda i: (i, 0)),
                  pl.BlockSpec((LANE, D), lambda i: (OFF_DT // LANE, 0))],
        out_specs=[pl.BlockSpec((tm, LANE), lambda i: (i, 0)),
                   pl.BlockSpec((2 * N_HEADS, tm), lambda i: (0, i))],
        out_shape=[jax.ShapeDtypeStruct((T, LANE), F32),
                   jax.ShapeDtypeStruct((2 * N_HEADS, T), F32)],
        compiler_params=_cp(("arbitrary",), 40),
        name="proj_dt",
    )(a, w_t)


def _conv_kernel(x_ref, w_ref, b_ref, o_ref):
    x = x_ref[0].astype(F32)
    n = x.shape[0]
    row = lax.broadcasted_iota(jnp.int32, x.shape, 0)
    acc = x * w_ref[CONV_K // 2:CONV_K // 2 + 1, :] + b_ref[...]
    for k in range(CONV_K):
        off = k - CONV_K // 2
        if off == 0:
            continue
        shifted = pltpu.roll(x, (-off) % n, 0)
        ok = (row + off >= 0) & (row + off < n)
        acc = acc + jnp.where(ok, shifted, 0.0) * w_ref[k:k + 1, :]
    o_ref[0] = _silu(acc).astype(o_ref.dtype)


def _conv_silu(zx3, conv_w, conv_b):
    tc = 512
    c0 = D_INNER // tc
    return pl.pallas_call(
        _conv_kernel,
        grid=(B, CONV_DIM // tc),
        in_specs=[pl.BlockSpec((1, S, tc), lambda b, j: (b, 0, c0 + j)),
                  pl.BlockSpec((CONV_K, tc), lambda b, j: (0, j)),
                  pl.BlockSpec((1, tc), lambda b, j: (0, j))],
        out_specs=pl.BlockSpec((1, S, tc), lambda b, j: (b, 0, j)),
        out_shape=jax.ShapeDtypeStruct((B, S, CONV_DIM), BF16),
        compiler_params=_cp(("arbitrary", "arbitrary"), 48),
        name="conv_silu",
    )(zx3, conv_w, conv_b)


def _ssd_direction(x_ref, y_ref, st_ref, dt_raw, dtT_raw, bias, biasT, alog, alogT, e_ref, dsk, reverse):
    tc = CHUNK
    gw = HEADS_PER_GROUP * HEAD_P
    dt = _softplus(dt_raw + bias)
    dtT = _softplus(dtT_raw + biasT)
    da = dt * (-jnp.exp(alog))
    daT = dtT * (-jnp.exp(alogT))
    r = lax.broadcasted_iota(jnp.int32, (tc, tc), 0)
    c = lax.broadcasted_iota(jnp.int32, (tc, tc), 1)
    keep = (c >= r) if reverse else (c <= r)
    tri = jnp.where(keep, 1.0, 0.0).astype(BF16)
    triT = jnp.where((r >= c) if reverse else (r <= c), 1.0, 0.0).astype(BF16)
    h3 = _split3(da)
    cs = _dot(tri, h3[0]) + _dot(tri, h3[1]) + _dot(tri, h3[2])
    t3 = _split3(daT)
    csT = _dot(t3[0], triT) + _dot(t3[1], triT) + _dot(t3[2], triT)
    tot = cs[0:1, :] if reverse else cs[tc - 1:tc, :]
    def hi_lo(v):
        h = v.astype(BF16).astype(F32)
        return jnp.concatenate([h, v - h], axis=1).astype(BF16)

    dt2 = hi_lo(dt)
    ed2 = hi_lo(jnp.exp(tot - cs))
    ec2 = hi_lo(jnp.exp(cs))
    et2 = hi_lo(jnp.broadcast_to(jnp.exp(tot), (16, N_HEADS)))
    lanehead = lax.shift_right_logical(lax.broadcasted_iota(jnp.int32, (tc, gw), 1), HEAD_P.bit_length() - 1)
    for g in range(N_GROUPS):
        eg = e_ref[:, g * gw:(g + 1) * gw]
        dt_e = _dot(dt2, eg)
        edec_e = _dot(ed2, eg)
        ecs_e = _dot(ec2, eg)
        etot_e = _dot(et2, eg)[0:1, :]
        xs = x_ref[:, g * gw:(g + 1) * gw].astype(F32)
        bg = x_ref[:, D_INNER + g * N_STATE:D_INNER + (g + 1) * N_STATE]
        cg = x_ref[:, D_INNER + N_GROUPS * N_STATE + g * N_STATE:D_INNER + N_GROUPS * N_STATE + (g + 1) * N_STATE]
        xd = xs * dt_e
        xdd = (xd * edec_e).astype(BF16)
        cb = _dot_nt(cg, bg)
        ms = []
        rs = []
        for k in range(HEADS_PER_GROUP):
            h = HEADS_PER_GROUP * g + k
            decay = jnp.where(keep, jnp.exp(cs[:, h:h + 1] - csT[h:h + 1, :]), 0.0)
            ms.append((cb * decay).astype(BF16))
            rs.append(jnp.where(lanehead == k, xd, 0.0).astype(BF16))
        lhs = jnp.concatenate(ms, axis=1)
        rhs = jnp.concatenate(rs, axis=0)
        sg = st_ref[g]
        y = _dot(lhs, rhs) + _dot(cg, sg.astype(BF16)) * ecs_e
        if dsk is not None:
            y = y + xs * dsk[:, g * gw:(g + 1) * gw]
        y_ref[:, g * gw:(g + 1) * gw] = y.astype(y_ref.dtype)
        bgT = bg.astype(F32).T.astype(BF16)
        st_ref[g] = sg * etot_e + _dot(bgT, xdd)


def _ssd_kernel(xf_ref, xb_ref, dtf_ref, dtb_ref, dtTf_ref, dtTb_ref, bias_ref, biasT_ref,
                alog_ref, alogT_ref, dsk_ref, e_ref, yf_ref, yb_ref, sf_ref, sb_ref):
    @pl.when(pl.program_id(1) == 0)
    def _():
        sf_ref[...] = jnp.zeros_like(sf_ref)
        sb_ref[...] = jnp.zeros_like(sb_ref)

    nh = N_HEADS
    _ssd_direction(xf_ref, yf_ref, sf_ref, dtf_ref[:, 0:nh], dtTf_ref[0:nh, :], bias_ref[:, 0:nh],
                   biasT_ref[0:nh, :], alog_ref[:, 0:nh], alogT_ref[0:nh, :], e_ref, dsk_ref[...], False)
    _ssd_direction(xb_ref, yb_ref, sb_ref, dtb_ref[:, nh:2 * nh], dtTb_ref[nh:2 * nh, :], bias_ref[:, nh:2 * nh],
                   biasT_ref[nh:2 * nh, :], alog_ref[:, nh:2 * nh], alogT_ref[nh:2 * nh, :], e_ref, None, True)


def _ssd(xbc, dt, dtT, bias, biasT, alog, alogT, dsk, expand):
    nc = N_CHUNKS
    fwd = lambda b, c: (b * nc + c, 0)
    bwd = lambda b, c: (b * nc + nc - 1 - c, 0)
    fwdT = lambda b, c: (0, b * nc + c)
    bwdT = lambda b, c: (0, b * nc + nc - 1 - c)
    const = lambda b, c: (0, 0)
    gw = HEADS_PER_GROUP * HEAD_P
    return pl.pallas_call(
        _ssd_kernel,
        grid=(B, nc),
        in_specs=[pl.BlockSpec((CHUNK, CONV_DIM), fwd),
                  pl.BlockSpec((CHUNK, CONV_DIM), bwd),
                  pl.BlockSpec((CHUNK, LANE), fwd),
                  pl.BlockSpec((CHUNK, LANE), bwd),
                  pl.BlockSpec((2 * N_HEADS, CHUNK), fwdT),
                  pl.BlockSpec((2 * N_HEADS, CHUNK), bwdT),
                  pl.BlockSpec((1, 2 * N_HEADS), const),
                  pl.BlockSpec((2 * N_HEADS, 1), const),
                  pl.BlockSpec((1, 2 * N_HEADS), const),
                  pl.BlockSpec((2 * N_HEADS, 1), const),
                  pl.BlockSpec((1, D_INNER), const),
                  pl.BlockSpec((2 * N_HEADS, D_INNER), const)],
        out_specs=[pl.BlockSpec((CHUNK, D_INNER), fwd),
                   pl.BlockSpec((CHUNK, D_INNER), bwd)],
        out_shape=[jax.ShapeDtypeStruct((T, D_INNER), BF16),
                   jax.ShapeDtypeStruct((T, D_INNER), BF16)],
        scratch_shapes=[pltpu.VMEM((N_GROUPS, N_STATE, gw), F32),
                        pltpu.VMEM((N_GROUPS, N_STATE, gw), F32)],
        compiler_params=_cp(("arbitrary", "arbitrary"), 40),
        name="ssd_scan",
    )(xbc, xbc, dt, dt, dtT, dtT, bias, biasT, alog, alogT, dsk, expand)


def _gnorm_kernel(yf_ref, yb_ref, z_ref, w_ref, o_ref):
    z = z_ref[...].astype(F32)
    y = (yf_ref[...].astype(F32) + yb_ref[...].astype(F32)) * _silu(z)
    ms = jnp.mean(y * y, axis=-1, keepdims=True)
    o_ref[...] = (y * lax.rsqrt(ms + EPS) * w_ref[...]).astype(o_ref.dtype)


def _gated_norm(yf, yb, zx, w):
    tm = 512
    return pl.pallas_call(
        _gnorm_kernel,
        grid=(T // tm,),
        in_specs=[pl.BlockSpec((tm, D_INNER), lambda i: (i, 0)),
                  pl.BlockSpec((tm, D_INNER), lambda i: (i, 0)),
                  pl.BlockSpec((tm, D_INNER), lambda i: (i, 0)),
                  pl.BlockSpec((1, D_INNER), lambda i: (0, 0))],
        out_specs=pl.BlockSpec((tm, D_INNER), lambda i: (i, 0)),
        out_shape=jax.ShapeDtypeStruct((T, D_INNER), BF16),
        compiler_params=_cp(("arbitrary",), 40),
        name="ssm_gated_norm",
    )(yf, yb, zx, w)


def _attn_kernel(q_ref, k_ref, v_ref, o_ref, lse_ref, *, length, hb, rb, radius):
    tq = 128
    win = tq + 2 * radius
    pair = length < win
    assert (not pair) or (2 * length == win and rb % 2 == 0)
    for rr in range(rb):
        for h in range(hb):
            cols = slice(h * A_HEAD, (h + 1) * A_HEAD)
            for qb in range(length // tq):
                q0 = qb * tq
                q = q_ref[0, rr, q0:q0 + tq, cols]
                if pair:
                    r0 = rr - rr % 2
                    ws = -(rr % 2) * length
                    k = k_ref[0, r0:r0 + 2, :, cols].reshape(win, A_HEAD)
                    v = v_ref[0, r0:r0 + 2, :, cols].reshape(win, A_HEAD)
                else:
                    ws = min(max(q0 - radius, 0), length - win)
                    k = k_ref[0, rr, ws:ws + win, cols]
                    v = v_ref[0, rr, ws:ws + win, cols]
                s = _dot_nt(q, k)
                ti = q0 + lax.broadcasted_iota(jnp.int32, (tq, win), 0)
                tj = ws + lax.broadcasted_iota(jnp.int32, (tq, win), 1)
                ok = (tj - ti <= radius) & (ti - tj <= radius)
                if pair:
                    ok = ok & (tj >= 0) & (tj < length)
                s = jnp.where(ok, s, NEG_INF)
                m = jnp.max(s, axis=-1, keepdims=True)
                p = jnp.exp(s - m)
                l = jnp.sum(p, axis=-1, keepdims=True)
                o = _dot(p.astype(BF16), v) / l
                o_ref[0, rr, q0:q0 + tq, cols] = o.astype(o_ref.dtype)
                lse_ref[0, rr, 0, q0:q0 + tq, h:h + 1] = m + jnp.log(l)


def _attention_group(qkv4, g, dil, hb, rb):
    length = S // dil
    window = PATTERNS[g][0]
    radius = window // (2 * dil)
    bw = hb * A_HEAD
    nj = A_HPG // hb
    return pl.pallas_call(
        functools.partial(_attn_kernel, length=length, hb=hb, rb=rb, radius=radius),
        grid=(B, dil // rb, nj),
        in_specs=[pl.BlockSpec((1, rb, length, bw), lambda b, r, j: (b, r, 0, j)),
                  pl.BlockSpec((1, rb, length, bw), lambda b, r, j: (b, r, 0, nj + j)),
                  pl.BlockSpec((1, rb, length, bw), lambda b, r, j: (b, r, 0, 2 * nj + j))],
        out_specs=[pl.BlockSpec((1, rb, length, bw), lambda b, r, j: (b, r, 0, j)),
                   pl.BlockSpec((1, rb, 1, length, hb), lambda b, r, j: (b, r, j, 0, 0))],
        out_shape=[jax.ShapeDtypeStruct((B, dil, length, A_OUT), BF16),
                   jax.ShapeDtypeStruct((B, dil, nj, length, hb), F32)],
        compiler_params=_cp(("arbitrary", "arbitrary", "arbitrary"), 40),
        name=f"attn_dil{dil}",
    )(qkv4, qkv4, qkv4)


def _combine_kernel(o0_ref, o1_ref, o2_ref, lse_ref, out_ref):
    lse = lse_ref[...]
    for hh in range(A_HPG):
        l0 = lse[:, hh:hh + 1]
        l1 = lse[:, A_HPG + hh:A_HPG + hh + 1]
        l2 = lse[:, 2 * A_HPG + hh:2 * A_HPG + hh + 1]
        m = jnp.maximum(jnp.maximum(l0, l1), l2)
        e0 = jnp.exp(l0 - m)
        e1 = jnp.exp(l1 - m)
        e2 = jnp.exp(l2 - m)
        inv = 1.0 / (e0 + e1 + e2)
        cols = slice(hh * A_HEAD, (hh + 1) * A_HEAD)
        mix = ((e0 * inv) * o0_ref[:, cols].astype(F32) + (e1 * inv) * o1_ref[:, cols].astype(F32)
               + (e2 * inv) * o2_ref[:, cols].astype(F32))
        out_ref[:, cols] = mix.astype(out_ref.dtype)


def _combine_groups(o0, o1, o2, lse):
    tm = 512
    n_l = lse.shape[1]
    return pl.pallas_call(
        _combine_kernel,
        grid=(T // tm,),
        in_specs=[pl.BlockSpec((tm, A_OUT), lambda i: (i, 0)),
                  pl.BlockSpec((tm, A_OUT), lambda i: (i, 0)),
                  pl.BlockSpec((tm, A_OUT), lambda i: (i, 0)),
                  pl.BlockSpec((tm, n_l), lambda i: (i, 0))],
        out_specs=pl.BlockSpec((tm, A_OUT), lambda i: (i, 0)),
        out_shape=jax.ShapeDtypeStruct((T, A_OUT), BF16),
        compiler_params=_cp(("arbitrary",), 40),
        name="attn_combine",
    )(o0, o1, o2, lse)


def _merge_kernel(ys_ref, oa_ref, ws_ref, wa_ref, gs_ref, ga_ref, o_ref, wsb_ref, wab_ref):
    @pl.when(pl.program_id(1) == 0)
    def _():
        wsb_ref[...] = ws_ref[...].astype(BF16)
        wab_ref[...] = wa_ref[...].astype(BF16)

    y_ssm = _dot(ys_ref[...], wsb_ref[...])
    y_att = _dot(oa_ref[...], wab_ref[...])
    g_s = jax.nn.sigmoid(gs_ref[...].astype(F32))
    g_a = jax.nn.sigmoid(ga_ref[...].astype(F32))
    o_ref[...] = (g_s * y_ssm + g_a * y_att).astype(o_ref.dtype)


def _merge(yn, oc, w_ssm_out, w_attn_out, gates):
    tm, tn = 1024, 512
    g0 = 0
    g1 = D // tn
    return pl.pallas_call(
        _merge_kernel,
        grid=(D // tn, T // tm),
        in_specs=[pl.BlockSpec((tm, D_INNER), lambda j, i: (i, 0)),
                  pl.BlockSpec((tm, A_OUT), lambda j, i: (i, 0)),
                  pl.BlockSpec((D_INNER, tn), lambda j, i: (0, j)),
                  pl.BlockSpec((A_OUT, tn), lambda j, i: (0, j)),
                  pl.BlockSpec((tm, tn), lambda j, i: (i, g0 + j)),
                  pl.BlockSpec((tm, tn), lambda j, i: (i, g1 + j))],
        out_specs=pl.BlockSpec((tm, tn), lambda j, i: (i, j)),
        out_shape=jax.ShapeDtypeStruct((T, D), BF16),
        scratch_shapes=[pltpu.VMEM((D_INNER, tn), BF16), pltpu.VMEM((A_OUT, tn), BF16)],
        compiler_params=_cp(("arbitrary", "arbitrary"), 48),
        name="branch_merge",
    )(yn, oc, w_ssm_out, w_attn_out, gates, gates)


def _wo_kernel(a_ref, w_ref, x_ref, g_ref, o_ref, wb_ref):
    @pl.when(pl.program_id(1) == 0)
    def _():
        wb_ref[...] = w_ref[...].astype(BF16)

    o_ref[...] = x_ref[...] + g_ref[0] * _dot(a_ref[...], wb_ref[...])


def _wo_residual(merged, w_o, x2d, gate1):
    tm, tn = 1024, 512
    per_b = S // tm
    return pl.pallas_call(
        _wo_kernel,
        grid=(D // tn, T // tm),
        in_specs=[pl.BlockSpec((tm, D), lambda j, i: (i, 0)),
                  pl.BlockSpec((D, tn), lambda j, i: (0, j)),
                  pl.BlockSpec((tm, tn), lambda j, i: (i, j)),
                  pl.BlockSpec((1, 1, tn), lambda j, i: (i // per_b, 0, j))],
        out_specs=pl.BlockSpec((tm, tn), lambda j, i: (i, j)),
        out_shape=jax.ShapeDtypeStruct((T, D), F32),
        scratch_shapes=[pltpu.VMEM((D, tn), BF16)],
        compiler_params=_cp(("arbitrary", "arbitrary"), 52),
        name="out_proj_residual",
    )(merged, w_o, x2d, gate1)


def _norm_router_kernel(x_ref, w_ref, sc_ref, sh_ref, wr_ref, h_ref, aff_ref):
    x = x_ref[...]
    ms = jnp.mean(x * x, axis=-1, keepdims=True)
    h = x * lax.rsqrt(ms + EPS) * w_ref[...] * (1.0 + sc_ref[0]) + sh_ref[0]
    h_ref[...] = h.astype(h_ref.dtype)
    hh, hl = _split2(h)
    wh, wl = _split2(wr_ref[...])
    logits = _dot(hh, wh) + _dot(hl, wh) + _dot(hh, wl)
    lane = lax.broadcasted_iota(jnp.int32, logits.shape, 1)
    logits = jnp.where(lane < N_EXPERTS, logits, NEG_INF)
    m = jnp.max(logits, axis=-1, keepdims=True)
    e = jnp.exp(logits - m)
    aff_ref[...] = e / jnp.sum(e, axis=-1, keepdims=True)


def _norm_router(x2d, w, sc, sh, wr_pad):
    tm = 256
    per_b = S // tm
    return pl.pallas_call(
        _norm_router_kernel,
        grid=(T // tm,),
        in_specs=[pl.BlockSpec((tm, D), lambda i: (i, 0)),
                  pl.BlockSpec((1, D), lambda i: (0, 0)),
                  pl.BlockSpec((1, 1, D), lambda i: (i // per_b, 0, 0)),
                  pl.BlockSpec((1, 1, D), lambda i: (i // per_b, 0, 0)),
                  pl.BlockSpec((D, LANE), lambda i: (0, 0))],
        out_specs=[pl.BlockSpec((tm, D), lambda i: (i, 0)),
                   pl.BlockSpec((tm, LANE), lambda i: (i, 0))],
        out_shape=[jax.ShapeDtypeStruct((T, D), BF16),
                   jax.ShapeDtypeStruct((T, LANE), F32)],
        compiler_params=_cp(("arbitrary",), 40),
        name="norm2_router",
    )(x2d, w, sc, sh, wr_pad)


def _topk_kernel(a_ref, at_ref, u_ref, slot_ref, w_ref, rank_ref):
    rb = 256
    a_tok = at_ref[0]
    lane = lax.broadcasted_iota(jnp.int32, a_tok.shape, 1)
    ones = jnp.ones((16, rb), BF16)

    def count_greater(e, carry):
        a_row = a_ref[0, pl.ds(e, 1), :]
        a_col = jnp.sum(jnp.where(lane == e, a_tok, 0.0), axis=1, keepdims=True)
        r = jnp.zeros((16, S), F32)
        for blk in range(S // rb):
            greater = jnp.where(a_col[blk * rb:(blk + 1) * rb] > a_row, 1.0, 0.0).astype(BF16)
            r = r + _dot(ones, greater)
        rank_ref[pl.ds(e, 1), :] = r[0:1]
        return carry

    lax.fori_loop(0, N_EXPERTS, count_greater, 0)
    a = a_ref[0]
    in_top = rank_ref[...] < CAP
    n_top = jnp.sum(jnp.where(in_top, 1.0, 0.0), axis=1, keepdims=True)
    v = jnp.min(jnp.where(in_top, a, jnp.inf), axis=1, keepdims=True)
    tie = a == v
    n_tie = jnp.sum(jnp.where(tie, 1.0, 0.0), axis=1, keepdims=True)
    need = CAP - (n_top - n_tie)
    u = u_ref[...]
    tie_rank = _dot(jnp.where(tie, 1.0, 0.0).astype(BF16), u)
    sel = (in_top & (a > v)) | (tie & (tie_rank < need))
    slot = _dot(jnp.where(sel, 1.0, 0.0).astype(BF16), u)
    slot_ref[0] = jnp.where(sel, slot.astype(jnp.int32), -1)
    w_ref[0] = jnp.where(sel, a, 0.0)


def _topk(aff_t, aff3, upper):
    return pl.pallas_call(
        _topk_kernel,
        grid=(B,),
        in_specs=[pl.BlockSpec((1, N_EXPERTS, S), lambda b: (b, 0, 0)),
                  pl.BlockSpec((1, S, LANE), lambda b: (b, 0, 0)),
                  pl.BlockSpec((S, S), lambda b: (0, 0))],
        out_specs=[pl.BlockSpec((1, N_EXPERTS, S), lambda b: (b, 0, 0)),
                   pl.BlockSpec((1, N_EXPERTS, S), lambda b: (b, 0, 0))],
        out_shape=[jax.ShapeDtypeStruct((B, N_EXPERTS, S), jnp.int32),
                   jax.ShapeDtypeStruct((B, N_EXPERTS, S), F32)],
        scratch_shapes=[pltpu.VMEM((N_EXPERTS, S), F32)],
        compiler_params=_cp(("arbitrary",), 48),
        name="expert_topk",
    )(aff_t, aff3, upper)


def _gather_kernel(slot_ref, h_ref, xe_ref):
    hblk = h_ref[0]
    j = lax.broadcasted_iota(jnp.int32, (CAP, S), 0)
    for e in range(N_EXPERTS):
        p = jnp.where(j == slot_ref[0, e:e + 1, :], 1.0, 0.0).astype(BF16)
        xe_ref[e, 0] = _dot(p, hblk).astype(xe_ref.dtype)


def _gather(slot, h3):
    td = 1024
    return pl.pallas_call(
        _gather_kernel,
        grid=(B, D // td),
        in_specs=[pl.BlockSpec((1, N_EXPERTS, S), lambda b, j: (b, 0, 0)),
                  pl.BlockSpec((1, S, td), lambda b, j: (b, 0, j))],
        out_specs=pl.BlockSpec((N_EXPERTS, 1, CAP, td), lambda b, j: (0, b, 0, j)),
        out_shape=jax.ShapeDtypeStruct((N_EXPERTS, B, CAP, D), BF16),
        compiler_params=_cp(("arbitrary", "arbitrary"), 48),
        name="expert_gather",
    )(slot, h3)


def _gateup_kernel(x_ref, wg_ref, wu_ref, a_ref):
    x = x_ref[0]
    g = _dot(x, wg_ref[0].astype(BF16))
    u = _dot(x, wu_ref[0].astype(BF16))
    a_ref[0] = (_silu(g) * u).astype(a_ref.dtype)


def _gateup(xe, w_gate, w_up):
    tf = 256
    m = B * CAP
    return pl.pallas_call(
        _gateup_kernel,
        grid=(N_EXPERTS, FF // tf),
        in_specs=[pl.BlockSpec((1, m, D), lambda e, f: (e, 0, 0)),
                  pl.BlockSpec((1, D, tf), lambda e, f: (e, 0, f)),
                  pl.BlockSpec((1, D, tf), lambda e, f: (e, 0, f))],
        out_specs=pl.BlockSpec((1, m, tf), lambda e, f: (e, 0, f)),
        out_shape=jax.ShapeDtypeStruct((N_EXPERTS, m, FF), BF16),
        compiler_params=_cp(("arbitrary", "arbitrary"), 48),
        name="expert_gate_up",
    )(xe, w_gate, w_up)


def _down_kernel(a_ref, w_ref, o_ref):
    o_ref[0] = _dot(a_ref[0], w_ref[0].astype(BF16)).astype(o_ref.dtype)


def _down(a, w_down):
    tn = 1024
    m = B * CAP
    return pl.pallas_call(
        _down_kernel,
        grid=(N_EXPERTS, D // tn),
        in_specs=[pl.BlockSpec((1, m, FF), lambda e, j: (e, 0, 0)),
                  pl.BlockSpec((1, FF, tn), lambda e, j: (e, 0, j))],
        out_specs=pl.BlockSpec((1, m, tn), lambda e, j: (e, 0, j)),
        out_shape=jax.ShapeDtypeStruct((N_EXPERTS, m, D), BF16),
        compiler_params=_cp(("arbitrary", "arbitrary"), 48),
        name="expert_down",
    )(a, w_down)


def _scatter_kernel(slot_t_ref, w_t_ref, o_ref, x_ref, g_ref, y_ref, pw_ref):
    @pl.when(pl.program_id(1) == 0)
    def _():
        lane = lax.broadcasted_iota(jnp.int32, (S, CAP), 1)
        st = slot_t_ref[0]
        wt = w_t_ref[0]
        for e in range(N_EXPERTS):
            hit = lane == st[:, e:e + 1]
            pw_ref[:, e * CAP:(e + 1) * CAP] = jnp.where(hit, wt[:, e:e + 1], 0.0).astype(BF16)

    o = o_ref[:, 0].reshape(N_EXPERTS * CAP, o_ref.shape[-1])
    y_ref[0] = x_ref[0] + g_ref[0] * _dot(pw_ref[...], o)


def _scatter_residual(slot_t, w_t, out4, x3, gate2):
    tn = 512
    return pl.pallas_call(
        _scatter_kernel,
        grid=(B, D // tn),
        in_specs=[pl.BlockSpec((1, S, N_EXPERTS), lambda b, j: (b, 0, 0)),
                  pl.BlockSpec((1, S, N_EXPERTS), lambda b, j: (b, 0, 0)),
                  pl.BlockSpec((N_EXPERTS, 1, CAP, tn), lambda b, j: (0, b, 0, j)),
                  pl.BlockSpec((1, S, tn), lambda b, j: (b, 0, j)),
                  pl.BlockSpec((1, 1, tn), lambda b, j: (b, 0, j))],
        out_specs=pl.BlockSpec((1, S, tn), lambda b, j: (b, 0, j)),
        out_shape=jax.ShapeDtypeStruct((B, S, D), F32),
        scratch_shapes=[pltpu.VMEM((S, N_EXPERTS * CAP), BF16)],
        compiler_params=_cp(("arbitrary", "arbitrary"), 56),
        name="expert_scatter",
    )(slot_t, w_t, out4, x3, gate2)


def _final_kernel(x_ref, w_ref, o_ref):
    x = x_ref[...]
    ms = jnp.mean(x * x, axis=-1, keepdims=True)
    o_ref[...] = x * lax.rsqrt(ms + EPS) * w_ref[...]


def _final(x2d, w):
    tm = 256
    return pl.pallas_call(
        _final_kernel,
        grid=(T // tm,),
        in_specs=[pl.BlockSpec((tm, D), lambda i: (i, 0)),
                  pl.BlockSpec((1, D), lambda i: (0, 0))],
        out_specs=pl.BlockSpec((tm, D), lambda i: (i, 0)),
        out_shape=jax.ShapeDtypeStruct((T, D), F32),
        compiler_params=_cp(("arbitrary",), 40),
        name="final_norm",
    )(x2d, w)


def _rope_tables():
    half = A_HEAD // 2
    inv_freq = ROPE_THETA ** (-jnp.arange(half, dtype=F32) / half)
    ang = jnp.arange(S, dtype=F32)[:, None] * inv_freq[None, :]
    cos = jnp.cos(ang)
    sin = jnp.sin(ang)
    cos2 = jnp.concatenate([cos, cos], axis=-1)
    sin2 = jnp.concatenate([-sin, sin], axis=-1)
    scale = A_HEAD ** -0.5
    return (jnp.stack([cos2 * scale, cos2, jnp.ones_like(cos2)]),
            jnp.stack([sin2 * scale, sin2, jnp.zeros_like(sin2)]))


def kernel(x, c, norm1_w, norm2_w, normf_w, w_ada, b_ada, w_in, conv_w, conv_b, dt_bias_f, dt_bias_b,
           a_log_f, a_log_b, d_skip, ssm_norm_w, w_ssm_out, w_attn_out, w_o, w_router, w_gate_e, w_up_e,
           w_down_e):
    assert x.shape == (B, S, D) and c.shape == (B, D) and w_in.shape[0] == 1
    x2d = x.reshape(T, D)
    layer = 0

    c8 = jnp.pad(c, ((0, 8 - B), (0, 0)))
    b_ada_l = b_ada[layer][None, :]
    n_early = 2 * D
    shift1, scale1 = [m[:, None, :] for m in jnp.split(_ada(c8, w_ada[layer], b_ada_l, n_early)[:B], 2, axis=-1)]

    dilations = tuple(d for _, d in PATTERNS)
    h1_all = _norm_mod(x2d, norm1_w[layer][None, :], scale1, shift1, dilations[1:])
    h1 = h1_all[0]

    w_t = jnp.swapaxes(w_in[layer], 0, 1)
    zx, w_qkvg = _proj_zx_and_repack(h1, w_t)
    dt, dt_t = _dt_proj(h1, w_t)
    cos_tab, sin_tab = _rope_tables()
    gates, mod_late = _mm_with_modulation(h1, w_qkvg, c8, w_ada[layer], b_ada_l, col0=3 * A_WIDTH, n=2 * D,
                                          mod_col0=n_early, tm=1024, tn=1024)
    gate1, shift2, scale2, gate2 = [m[:, None, :] for m in jnp.split(mod_late[:B], 4, axis=-1)]

    xbc = _conv_silu(zx.reshape(B, S, OFF_DT), conv_w[layer], conv_b[layer][None, :]).reshape(T, CONV_DIM)
    bias = jnp.concatenate([dt_bias_f[layer], dt_bias_b[layer]])
    alog = jnp.concatenate([a_log_f[layer], a_log_b[layer]])
    dsk = jnp.repeat(d_skip[layer], HEAD_P)[None, :]
    expand = (jnp.arange(D_INNER)[None, :] // HEAD_P == jnp.arange(N_HEADS)[:, None]).astype(BF16)
    expand = jnp.concatenate([expand, expand], axis=0)
    yf, yb = _ssd(xbc, dt, dt_t, bias[None, :], bias[:, None], alog[None, :], alog[:, None], dsk, expand)
    yn = _gated_norm(yf, yb, zx, ssm_norm_w[layer][None, :])

    outs, lses = [], []
    for g, dil in enumerate(dilations):
        length = S // dil
        hb, rb = {1: (4, 1), 4: (8, 2), 16: (8, 8)}[dil]
        to_group_order = lambda t: t.reshape(3, length, dil, A_HEAD).transpose(0, 2, 1, 3).reshape(3, S, A_HEAD)
        qkv_g = _qkv_group(h1_all[g].reshape(T, D), w_qkvg, to_group_order(cos_tab), to_group_order(sin_tab), g)
        o_g, lse_g = _attention_group(qkv_g.reshape(B, dil, length, 3 * A_OUT), g, dil, hb, rb)
        outs.append(o_g.transpose(0, 2, 1, 3).reshape(T, A_OUT))
        lses.append(lse_g.transpose(0, 3, 1, 2, 4).reshape(T, A_HPG))
    oc = _combine_groups(outs[0], outs[1], outs[2], jnp.concatenate(lses, axis=-1))

    merged = _merge(yn, oc, w_ssm_out[layer], w_attn_out[layer], gates)
    x1 = _wo_residual(merged, w_o[layer], x2d, gate1)

    wr_pad = jnp.pad(w_router[layer], ((0, 0), (0, LANE - N_EXPERTS)))
    h2, aff = _norm_router(x1, norm2_w[layer][None, :], scale2, shift2, wr_pad)
    aff_t = aff[:, :N_EXPERTS].reshape(B, S, N_EXPERTS).transpose(0, 2, 1)
    upper = (jnp.arange(S)[:, None] < jnp.arange(S)[None, :]).astype(BF16)
    slot, wsel = _topk(aff_t, aff.reshape(B, S, LANE), upper)
    xe = _gather(slot, h2.reshape(B, S, D))
    act = _gateup(xe.reshape(N_EXPERTS, B * CAP, D), w_gate_e[layer], w_up_e[layer])
    out_e = _down(act, w_down_e[layer])
    x2 = _scatter_residual(slot.transpose(0, 2, 1), wsel.transpose(0, 2, 1),
                           out_e.reshape(N_EXPERTS, B, CAP, D), x1.reshape(B, S, D), gate2)

    out = _final(x2.reshape(T, D), normf_w[None, :])
    return out.reshape(B, S, D)
```
